```python
import functools
import jax, jax.numpy as jnp
from jax import lax
import numpy as np

D_MODEL = 1024
BATCH = 32
SEQ = 2048
DEPTH = 1
DEC_BATCH = 128
DEC_SEQ = 8
PAST_LEN = 16384
PAGE_SIZE = 128

RW_HEADS = 8
RW_HEAD_DIM = 64
RW_WIDTH = RW_HEADS * RW_HEAD_DIM
W_LORA = 64
A_LORA = 64
G_LORA = 128
RW_PROJ = 3 * RW_WIDTH + W_LORA + A_LORA + G_LORA
RW_SPLITS = [RW_WIDTH, 2 * RW_WIDTH, 3 * RW_WIDTH, 3 * RW_WIDTH + W_LORA, 3 * RW_WIDTH + W_LORA + A_LORA]
GN_EPS = 64e-5
L2_EPS = 1e-12
MLA_HEADS = 4
QK_NOPE = 128
QK_ROPE = 64
V_HEAD = 128
Q_RANK = 384
KV_RANK = 256
MLA_WIDTH = MLA_HEADS * V_HEAD
MLA_PROJ = Q_RANK + KV_RANK + QK_ROPE
ROPE_THETA = 10000.0
Q_BLOCK = 128
MIX_WIDTH = RW_WIDTH + MLA_WIDTH
IN_PROJ = RW_PROJ + MLA_PROJ
MEM_TOKENS = 256
X_HEADS = 4
X_HEAD_DIM = 128
X_WIDTH = X_HEADS * X_HEAD_DIM
N_EXPERTS = 32
TOP_K = 4
D_FF = 1024
SWIGLU_LIMIT = 7.0
SWIGLU_ALPHA = 1.702
MOE_BLOCK = 128
NORM_EPS = 1e-5
POOL_NUM = 5
POOL_DEN = 4

kernel_name = 'hymba_rwkv7_mla_moe_step'


def rmsnorm(x, g):
    xf = x.astype(jnp.float32)
    y = xf * lax.rsqrt(jnp.mean(xf * xf, axis=-1, keepdims=True) + NORM_EPS)
    return (y * g.astype(jnp.float32)).astype(x.dtype)


def rope(x, pos):
    half = QK_ROPE // 2
    inv = ROPE_THETA ** (-jnp.arange(half, dtype=jnp.float32) / half)
    ang = pos.astype(jnp.float32)[:, None] * inv[None, :]
    cos = jnp.cos(ang)[None, :, None, :]
    sin = jnp.sin(ang)[None, :, None, :]
    xf = x.astype(jnp.float32)
    x1, x2 = xf[..., :half], xf[..., half:]
    return jnp.concatenate([x1 * cos - x2 * sin, x1 * sin + x2 * cos], axis=-1).astype(x.dtype)


def rwkv_scan(state0, r, w, k, v, a, b):
    def step(S, inp):
        r_t, w_t, k_t, v_t, a_t, b_t = inp
        sa = jnp.einsum('bhvk,bhk->bhv', S, a_t)
        S = S * w_t[:, :, None, :] + sa[..., None] * b_t[:, :, None, :] + v_t[..., None] * k_t[:, :, None, :]
        return S, jnp.einsum('bhvk,bhk->bhv', S, r_t)
    xs = tuple(jnp.moveaxis(t, 1, 0) for t in (r, w, k, v, a, b))
    S, ys = lax.scan(step, state0, xs)
    return jnp.moveaxis(ys, 0, 1), S


def rwkv_group(proj_r, shift_prev, state0, mu_shift, w0, w2, a0, a2, g2, k_k, k_a, r_k, lnx_w, lnx_b):
    B, T, _ = proj_r.shape
    f32 = jnp.float32
    prev = jnp.concatenate([shift_prev[:, None, :].astype(proj_r.dtype), proj_r[:, :-1]], axis=1)
    mixed = proj_r + (prev - proj_r) * mu_shift
    r, k, v, wd, ad, gd = jnp.split(mixed, RW_SPLITS, axis=-1)
    w_log = -jax.nn.softplus(-(w0 + jnp.tanh(wd) @ w2).astype(f32)) - 0.5
    decay = jnp.exp(-jnp.exp(w_log))
    a = jax.nn.sigmoid((a0 + ad @ a2).astype(f32))
    g = (jax.nn.sigmoid(gd) @ g2).astype(f32)

    def heads(t):
        return t.astype(f32).reshape(t.shape[:-1] + (RW_HEADS, RW_HEAD_DIM))

    r_h, k_h, v_h, a_h, w_h = heads(r), heads(k), heads(v), heads(a), heads(decay)
    kk = k_h * heads(k_k)
    kk = kk * lax.rsqrt(jnp.sum(kk * kk, axis=-1, keepdims=True) + L2_EPS)
    k_h = k_h * (1.0 + (a_h - 1.0) * heads(k_a))
    y, state = rwkv_scan(state0.astype(f32), r_h, w_h, k_h, v_h, -kk, kk * a_h)
    mean = jnp.mean(y, axis=-1, keepdims=True)
    var = jnp.mean(jnp.square(y - mean), axis=-1, keepdims=True)
    yn = (y - mean) * lax.rsqrt(var + GN_EPS) * heads(lnx_w) + heads(lnx_b)
    bonus = jnp.sum(r_h * k_h * r_k.astype(f32), axis=-1, keepdims=True) * v_h
    out = ((yn + bonus) * heads(g)).reshape(B, T, RW_WIDTH)
    return out.astype(proj_r.dtype), state, proj_r[:, -1]


def mla_project(proj_m, pos, q_norm, w_qb, kv_norm):
    B, T, _ = proj_m.shape
    q_a = proj_m[..., :Q_RANK]
    ckv = rmsnorm(proj_m[..., Q_RANK:Q_RANK + KV_RANK], kv_norm)
    k_rope = rope(proj_m[..., Q_RANK + KV_RANK:][:, :, None, :], pos)[:, :, 0, :]
    q = (rmsnorm(q_a, q_norm) @ w_qb).reshape(B, T, MLA_HEADS, QK_NOPE + QK_ROPE)
    return q[..., :QK_NOPE], rope(q[..., QK_NOPE:], pos), ckv, k_rope


def mla_prompt_attention(q_lat, q_rope, ckv, k_rope):
    B, T = q_lat.shape[:2]
    nb = T // Q_BLOCK
    scale = (QK_NOPE + QK_ROPE) ** -0.5
    kpos = jnp.arange(T)
    ql = q_lat.reshape(B, nb, Q_BLOCK, MLA_HEADS, KV_RANK).transpose(1, 0, 2, 3, 4)
    qr = q_rope.reshape(B, nb, Q_BLOCK, MLA_HEADS, QK_ROPE).transpose(1, 0, 2, 3, 4)

    def block(args):
        i, ql_b, qr_b = args
        s = (jnp.einsum('bqhr,bkr->bhqk', ql_b, ckv) + jnp.einsum('bqhe,bke->bhqk', qr_b, k_rope)).astype(jnp.float32) * scale
        qpos = i * Q_BLOCK + jnp.arange(Q_BLOCK)
        s = jnp.where(kpos[None, :] <= qpos[:, None], s, -jnp.inf)
        p = jax.nn.softmax(s, axis=-1).astype(ckv.dtype)
        return jnp.einsum('bhqk,bkr->bqhr', p, ckv)

    o = lax.map(block, (jnp.arange(nb), ql, qr))
    return o.transpose(1, 0, 2, 3, 4).reshape(B, T, MLA_HEADS, KV_RANK)


def mla_sample_attention(q_lat, q_rope, ckv, k_rope, pool_ckv, pool_krope, layer, page_table):
    B, T = q_lat.shape[:2]
    scale = (QK_NOPE + QK_ROPE) ** -0.5
    qlf = q_lat.astype(jnp.float32)
    qrf = q_rope.astype(jnp.float32)

    def scores(ck, kr):
        return (jnp.einsum('bqhr,bkr->bhqk', qlf, ck.astype(jnp.float32))
                + jnp.einsum('bqhe,bke->bhqk', qrf, kr.astype(jnp.float32))) * scale

    def merge(carry, s, ck):
        m, l, acc = carry
        m_new = jnp.maximum(m, jnp.max(s, axis=-1))
        p = jnp.exp(s - m_new[..., None])
        corr = jnp.exp(m - m_new)
        l = l * corr + jnp.sum(p, axis=-1)
        acc = acc * corr[..., None] + jnp.einsum('bhqk,bkr->bhqr', p, ck.astype(jnp.float32))
        return (m_new, l, acc)

    def page_step(carry, idx):
        ck = pool_ckv[layer, idx]
        kr = pool_krope[layer, idx]
        return merge(carry, scores(ck, kr), ck), None

    init = (jnp.full((B, MLA_HEADS, T), -1e30, jnp.float32),
            jnp.zeros((B, MLA_HEADS, T), jnp.float32),
            jnp.zeros((B, MLA_HEADS, T, KV_RANK), jnp.float32))
    carry, _ = lax.scan(page_step, init, page_table.T)
    causal = jnp.tril(jnp.ones((T, T), dtype=bool))
    s_new = jnp.where(causal[None, None], scores(ckv, k_rope), -jnp.inf)
    m, l, acc = merge(carry, s_new, ckv)
    o = acc / l[..., None]
    return jnp.transpose(o, (0, 2, 1, 3)).astype(q_lat.dtype)


def token_mixer(x, pos, shift_prev, rw_state0, mla_attend, norm_mix, w_in, mu_shift, rw_w0, rw_w2, rw_a0, rw_a2,
                rw_g2, rw_k_k, rw_k_a, rw_r_k, rw_lnx_w, rw_lnx_b, mla_q_norm, mla_w_qb, mla_kv_norm, mla_w_kvb,
                mla_out_norm, w_out):
    B, T, _ = x.shape
    proj = rmsnorm(x, norm_mix) @ w_in
    y_rw, rw_state, new_shift = rwkv_group(proj[..., :RW_PROJ], shift_prev, rw_state0, mu_shift, rw_w0, rw_w2,
                                           rw_a0, rw_a2, rw_g2, rw_k_k, rw_k_a, rw_r_k, rw_lnx_w, rw_lnx_b)
    q_nope, q_rope, ckv, k_rope = mla_project(proj[..., RW_PROJ:], pos, mla_q_norm, mla_w_qb, mla_kv_norm)
    w_kv = mla_w_kvb.reshape(KV_RANK, MLA_HEADS, QK_NOPE + V_HEAD)
    q_lat = jnp.einsum('bthd,rhd->bthr', q_nope, w_kv[..., :QK_NOPE])
    o_lat = mla_attend(q_lat, q_rope, ckv, k_rope)
    y_mla = jnp.einsum('bthr,rhd->bthd', o_lat, w_kv[..., QK_NOPE:]).reshape(B, T, MLA_WIDTH)
    y_mla = rmsnorm(y_mla, mla_out_norm)
    y = jnp.concatenate([y_rw, y_mla], axis=-1) @ w_out
    return x + y, ckv, k_rope, rw_state, new_shift


def memory_kv(mem, norm_mem, wk, wv):
    B, M, _ = mem.shape
    mn = rmsnorm(mem, norm_mem)
    return (mn @ wk).reshape(B, M, X_HEADS, X_HEAD_DIM), (mn @ wv).reshape(B, M, X_HEADS, X_HEAD_DIM)


def cross_attention(x, mem_k, mem_v, norm_x, wq, wo):
    B, T, _ = x.shape
    q = (rmsnorm(x, norm_x) @ wq).reshape(B, T, X_HEADS, X_HEAD_DIM)
    s = jnp.einsum('bthd,bmhd->bhtm', q, mem_k.astype(q.dtype)).astype(jnp.float32) * X_HEAD_DIM ** -0.5
    p = jax.nn.softmax(s, axis=-1).astype(q.dtype)
    o = jnp.einsum('bhtm,bmhd->bthd', p, mem_v.astype(q.dtype)).reshape(B, T, X_WIDTH)
    return x + o @ wo


def moe_block(x, norm_ffn, router_w, router_b, w_gate, b_gate, w_up, b_up, w_down, b_down):
    B, T, D = x.shape
    n_tok = B * T
    xn = rmsnorm(x, norm_ffn).reshape(n_tok, D)
    logits = (xn @ router_w + router_b).astype(jnp.float32)
    top_v, top_e = lax.top_k(logits, TOP_K)
    gates = jax.nn.softmax(top_v, axis=-1)
    n_assign = n_tok * TOP_K
    e_flat = top_e.reshape(-1).astype(jnp.int32)
    order = jnp.argsort(e_flat).astype(jnp.int32)
    e_sorted = e_flat[order]
    tok_sorted = order // TOP_K
    counts = jnp.zeros((N_EXPERTS,), jnp.int32).at[e_flat].add(1)
    padded = (counts + MOE_BLOCK - 1) // MOE_BLOCK * MOE_BLOCK
    pad_end = jnp.cumsum(padded).astype(jnp.int32)
    pad_start = pad_end - padded
    raw_start = jnp.cumsum(counts).astype(jnp.int32) - counts
    dest_sorted = pad_start[e_sorted] + (jnp.arange(n_assign, dtype=jnp.int32) - raw_start[e_sorted])
    n_blocks = -(-n_assign // MOE_BLOCK) + N_EXPERTS
    cap = n_blocks * MOE_BLOCK
    row_tok = jnp.full((cap,), n_tok, jnp.int32).at[dest_sorted].set(tok_sorted)
    x_pad = jnp.concatenate([xn, jnp.zeros((1, D), xn.dtype)], axis=0)
    x_disp = x_pad[row_tok].reshape(n_blocks, MOE_BLOCK, D)
    blk_e = jnp.minimum(jnp.searchsorted(pad_end, jnp.arange(n_blocks, dtype=jnp.int32) * MOE_BLOCK, side='right'),
                        N_EXPERTS - 1)

    def expert_block(args):
        xb, e = args
        gate = jnp.minimum(xb @ w_gate[e] + b_gate[e], SWIGLU_LIMIT)
        up = jnp.clip(xb @ w_up[e] + b_up[e], -SWIGLU_LIMIT, SWIGLU_LIMIT)
        h = (up + 1.0) * (gate * jax.nn.sigmoid(SWIGLU_ALPHA * gate))
        return h @ w_down[e] + b_down[e]

    y_disp = lax.map(expert_block, (x_disp, blk_e)).reshape(cap, D)
    dest_flat = jnp.zeros((n_assign,), jnp.int32).at[order].set(dest_sorted)
    y_sel = y_disp[dest_flat].reshape(n_tok, TOP_K, D)
    y = jnp.einsum('tk,tkd->td', gates.astype(y_sel.dtype), y_sel)
    return x + y.reshape(B, T, D)


def setup_inputs(seed: int = 0) -> dict:
    key = jax.random.key(seed)
    ks = iter(jax.random.split(key, 64))
    f32 = jnp.float32

    def nrm(shape, scale):
        return jax.random.normal(next(ks), shape, f32) * scale

    def gain(shape):
        return 1.0 + nrm(shape, 0.05)

    L, D = DEPTH, D_MODEL
    n_pages = PAST_LEN // PAGE_SIZE
    n_pool = (DEC_BATCH * n_pages * POOL_NUM) // POOL_DEN
    page_table = jax.random.permutation(next(ks), n_pool)[:DEC_BATCH * n_pages].reshape(DEC_BATCH, n_pages).astype(jnp.int32)
    return {
        'x_prompt': nrm((BATCH, SEQ, D), 1.0),
        'x_sample': nrm((DEC_BATCH, DEC_SEQ, D), 1.0),
        'mem_prompt': nrm((BATCH, MEM_TOKENS, D), 1.0),
        'cache_ckv': nrm((L, n_pool, PAGE_SIZE, KV_RANK), 1.0),
        'cache_krope': nrm((L, n_pool, PAGE_SIZE, QK_ROPE), 1.0),
        'cache_mem_k': nrm((L, DEC_BATCH, MEM_TOKENS, X_HEADS, X_HEAD_DIM), 1.0),
        'cache_mem_v': nrm((L, DEC_BATCH, MEM_TOKENS, X_HEADS, X_HEAD_DIM), 1.0),
        'state_rwkv': nrm((L, DEC_BATCH, RW_HEADS, RW_HEAD_DIM, RW_HEAD_DIM), 0.3),
        'state_shift': nrm((L, DEC_BATCH, RW_PROJ), 1.0),
        'page_table': page_table,
        'norm_mix': gain((L, D)),
        'w_in': nrm((L, D, IN_PROJ), D ** -0.5),
        'mu_shift': jax.random.uniform(next(ks), (L, RW_PROJ), f32),
        'rw_w0': nrm((L, RW_WIDTH), 0.5),
        'rw_w2': nrm((L, W_LORA, RW_WIDTH), W_LORA ** -0.5),
        'rw_a0': nrm((L, RW_WIDTH), 0.1),
        'rw_a2': nrm((L, A_LORA, RW_WIDTH), A_LORA ** -0.5),
        'rw_g2': nrm((L, G_LORA, RW_WIDTH), G_LORA ** -0.5),
        'rw_k_k': 0.85 + nrm((L, RW_WIDTH), 0.05),
        'rw_k_a': gain((L, RW_WIDTH)),
        'rw_r_k': nrm((L, RW_HEADS, RW_HEAD_DIM), 0.1),
        'rw_lnx_w': gain((L, RW_WIDTH)),
        'rw_lnx_b': nrm((L, RW_WIDTH), 0.01),
        'mla_q_norm': gain((L, Q_RANK)),
        'mla_w_qb': nrm((L, Q_RANK, MLA_HEADS * (QK_NOPE + QK_ROPE)), Q_RANK ** -0.5),
        'mla_kv_norm': gain((L, KV_RANK)),
        'mla_w_kvb': nrm((L, KV_RANK, MLA_HEADS * (QK_NOPE + V_HEAD)), KV_RANK ** -0.5),
        'mla_out_norm': gain((L, MLA_WIDTH)),
        'w_out': nrm((L, MIX_WIDTH, D), MIX_WIDTH ** -0.5),
        'norm_x': gain((L, D)),
        'norm_mem': gain((L, D)),
        'xa_wq': nrm((L, D, X_WIDTH), D ** -0.5),
        'xa_wk': nrm((L, D, X_WIDTH), D ** -0.5),
        'xa_wv': nrm((L, D, X_WIDTH), D ** -0.5),
        'xa_wo': nrm((L, X_WIDTH, D), X_WIDTH ** -0.5),
        'norm_ffn': gain((L, D)),
        'router_w': nrm((L, D, N_EXPERTS), D ** -0.5),
        'router_b': nrm((L, N_EXPERTS), 0.01),
        'moe_w_gate': nrm((L, N_EXPERTS, D, D_FF), D ** -0.5),
        'moe_b_gate': nrm((L, N_EXPERTS, D_FF), 0.01),
        'moe_w_up': nrm((L, N_EXPERTS, D, D_FF), D ** -0.5),
        'moe_b_up': nrm((L, N_EXPERTS, D_FF), 0.01),
        'moe_w_down': nrm((L, N_EXPERTS, D_FF, D), D_FF ** -0.5),
        'moe_b_down': nrm((L, N_EXPERTS, D), 0.01),
        'norm_final': gain((D,)),
    }


def reference(x_prompt, x_sample, mem_prompt, cache_ckv, cache_krope, cache_mem_k, cache_mem_v, state_rwkv,
              state_shift, page_table, norm_mix, w_in, mu_shift, rw_w0, rw_w2, rw_a0, rw_a2, rw_g2, rw_k_k, rw_k_a,
              rw_r_k, rw_lnx_w, rw_lnx_b, mla_q_norm, mla_w_qb, mla_kv_norm, mla_w_kvb, mla_out_norm, w_out, norm_x,
              norm_mem, xa_wq, xa_wk, xa_wv, xa_wo, norm_ffn, router_w, router_b, moe_w_gate, moe_b_gate, moe_w_up,
              moe_b_up, moe_w_down, moe_b_down, norm_final):
    b_p = x_prompt.shape[0]
    pos_p = jnp.arange(x_prompt.shape[1], dtype=jnp.int32)
    pos_s = PAST_LEN + jnp.arange(x_sample.shape[1], dtype=jnp.int32)
    hp, hs = x_prompt, x_sample
    ckv_p_l, kr_p_l, mk_p_l, mv_p_l, rw_p_l, sh_p_l = [], [], [], [], [], []
    ckv_s_l, kr_s_l, rw_s_l, sh_s_l = [], [], [], []
    for l in range(DEPTH):
        mix_w = (norm_mix[l], w_in[l], mu_shift[l], rw_w0[l], rw_w2[l], rw_a0[l], rw_a2[l], rw_g2[l], rw_k_k[l],
                 rw_k_a[l], rw_r_k[l], rw_lnx_w[l], rw_lnx_b[l], mla_q_norm[l], mla_w_qb[l], mla_kv_norm[l],
                 mla_w_kvb[l], mla_out_norm[l], w_out[l])
        shift0 = jnp.zeros((b_p, RW_PROJ), x_prompt.dtype)
        rw0 = jnp.zeros((b_p, RW_HEADS, RW_HEAD_DIM, RW_HEAD_DIM), jnp.float32)
        hp, ckv_p, kr_p, rw_p, sh_p = token_mixer(hp, pos_p, shift0, rw0, mla_prompt_attention, *mix_w)
        attend_s = functools.partial(mla_sample_attention, pool_ckv=cache_ckv, pool_krope=cache_krope, layer=l,
                                     page_table=page_table)
        hs, ckv_s, kr_s, rw_s, sh_s = token_mixer(hs, pos_s, state_shift[l], state_rwkv[l], attend_s, *mix_w)
        mk_p, mv_p = memory_kv(mem_prompt, norm_mem[l], xa_wk[l], xa_wv[l])
        hp = cross_attention(hp, mk_p, mv_p, norm_x[l], xa_wq[l], xa_wo[l])
        hs = cross_attention(hs, cache_mem_k[l], cache_mem_v[l], norm_x[l], xa_wq[l], xa_wo[l])
        moe_w = (norm_ffn[l], router_w[l], router_b[l], moe_w_gate[l], moe_b_gate[l], moe_w_up[l], moe_b_up[l],
                 moe_w_down[l], moe_b_down[l])
        hp = moe_block(hp, *moe_w)
        hs = moe_block(hs, *moe_w)
        ckv_p_l.append(ckv_p); kr_p_l.append(kr_p); mk_p_l.append(mk_p); mv_p_l.append(mv_p)
        rw_p_l.append(rw_p); sh_p_l.append(sh_p)
        ckv_s_l.append(ckv_s); kr_s_l.append(kr_s); rw_s_l.append(rw_s); sh_s_l.append(sh_s)
    y_prompt = rmsnorm(hp, norm_final)
    y_sample = rmsnorm(hs, norm_final)
    return (y_prompt, y_sample, jnp.stack(ckv_p_l), jnp.stack(kr_p_l), jnp.stack(mk_p_l), jnp.stack(mv_p_l),
            jnp.stack(rw_p_l), jnp.stack(sh_p_l), jnp.stack(ckv_s_l), jnp.stack(kr_s_l), jnp.stack(rw_s_l),
            jnp.stack(sh_s_l))
```

```python
import functools

import jax
import jax.numpy as jnp
from jax import lax
from jax.experimental import pallas as pl
from jax.experimental.pallas import tpu as pltpu

F32 = jnp.float32
BF16 = jnp.bfloat16
I32 = jnp.int32

D_MODEL = 1024
PAGE_SIZE = 128
RW_HEADS = 8
RW_HEAD_DIM = 64
RW_WIDTH = 512
RW_PROJ = 1792
GN_EPS = 64e-5
L2_EPS = 1e-12
MLA_HEADS = 4
QK_NOPE = 128
QK_ROPE = 64
V_HEAD = 128
Q_RANK = 384
KV_RANK = 256
ROPE_THETA = 10000.0
MEM_TOKENS = 256
X_HEADS = 4
X_HEAD_DIM = 128
X_WIDTH = 512
N_EXPERTS = 32
TOP_K = 4
SWIGLU_LIMIT = 7.0
SWIGLU_ALPHA = 1.702
NORM_EPS = 1e-5
MLA_SCALE = (QK_NOPE + QK_ROPE) ** -0.5
X_SCALE = X_HEAD_DIM ** -0.5
NEG_BIG = -1e30

KCAT = KV_RANK + 2 * 128
RW_CHUNK = 64
VMEM_LIMIT_V7X = 56 * 1024 * 1024


def _cparams(sem, vmem_mib=None):
    kw = dict(dimension_semantics=sem)
    if vmem_mib is not None:
        kw["vmem_limit_bytes"] = min(vmem_mib * 1024 * 1024, VMEM_LIMIT_V7X)
    return pltpu.CompilerParams(**kw)


def _nn(a, b):
    return jnp.dot(a, b, preferred_element_type=F32)


def _nt(a, b):
    return lax.dot_general(a, b, (((1,), (1,)), ((), ())), preferred_element_type=F32)


def _tn(a, b):
    return lax.dot_general(a, b, (((0,), (0,)), ((), ())), preferred_element_type=F32)


def _split2(x):
    hi = x.astype(BF16)
    lo = (x - hi.astype(F32)).astype(BF16)
    return hi, lo


def _split3(x):
    p1 = x.astype(BF16)
    r1 = x - p1.astype(F32)
    p2 = r1.astype(BF16)
    p3 = (r1 - p2.astype(F32)).astype(BF16)
    return p1, p2, p3


def _rms(x, g, eps=NORM_EPS):
    return x * lax.rsqrt(jnp.mean(x * x, axis=-1, keepdims=True) + eps) * g


def _sigmoid(x):
    return 1.0 / (1.0 + jnp.exp(-x))


def _full(shape):
    n = len(shape)
    return pl.BlockSpec(shape, lambda *a: (0,) * n)


def _mix_in_kernel(x_ref, nm_ref, wr_ref, wm_ref, qn_ref, wqb_ref, kvn_ref, wk_ref, c4_ref, s4_ref,
                   projr_ref, qcat_ref, kcat_ref, ckv_ref, krope_ref):
    xn = _rms(x_ref[...], nm_ref[...]).astype(BF16)
    projr_ref[...] = _nn(xn, wr_ref[...])
    pm = _nn(xn, wm_ref[...])
    q_a = pm[:, :Q_RANK]
    lat = pm[:, Q_RANK:Q_RANK + KV_RANK]
    k1 = pm[:, 640:768]
    k2 = pm[:, 768:896]
    c4 = c4_ref[...]
    s4 = s4_ref[...]
    ckv = _rms(lat, kvn_ref[...])
    ckv_ref[...] = ckv
    ok1 = k1 * c4 - k2 * s4
    ok2 = k1 * s4 + k2 * c4
    krope_ref[...] = jnp.concatenate([ok1[:, :32], ok2[:, :32]], axis=1)
    kcat_ref[...] = jnp.concatenate([ckv, ok1, ok2], axis=1).astype(BF16)
    qn = _rms(q_a, qn_ref[...]).astype(BF16)
    q = _nn(qn, wqb_ref[...]) * MLA_SCALE
    r1 = q[:, 512:640]
    r2 = q[:, 640:768]
    o1 = r1 * c4 - r2 * s4
    o2 = r1 * s4 + r2 * c4
    lane = lax.broadcasted_iota(I32, o1.shape, 1)
    for h in range(MLA_HEADS):
        ql = _nn(q[:, 128 * h:128 * h + 128].astype(BF16), wk_ref[h])
        mh = (lane >= 32 * h) & (lane < 32 * h + 32)
        qcat_ref[:, KCAT * h:KCAT * (h + 1)] = jnp.concatenate(
            [ql, jnp.where(mh, o1, 0.0), jnp.where(mh, o2, 0.0)], axis=1).astype(BF16)


def _mix_in(x2d, seq_len, pos0, pw, tm):
    n = x2d.shape[0]
    half = QK_ROPE // 2
    inv = ROPE_THETA ** (-jnp.arange(half, dtype=F32) / half)
    pos = (pos0 + jnp.arange(seq_len, dtype=jnp.int32)).astype(F32)
    ang = pos[:, None] * inv[None, :]
    tab_len = max(seq_len, tm)
    c4 = jnp.tile(jnp.cos(ang), (tab_len // seq_len, 4))
    s4 = jnp.tile(jnp.sin(ang), (tab_len // seq_len, 4))
    ntab = tab_len // tm
    row = lambda i: (i, 0)
    tab = lambda i: (i % ntab, 0)
    outs = pl.pallas_call(
        _mix_in_kernel,
        grid=(n // tm,),
        in_specs=[pl.BlockSpec((tm, D_MODEL), row), _full((1, D_MODEL)), _full((D_MODEL, RW_PROJ)),
                  _full((D_MODEL, 896)), _full((1, Q_RANK)), _full((Q_RANK, 768)), _full((1, KV_RANK)),
                  _full((MLA_HEADS, QK_NOPE, KV_RANK)), pl.BlockSpec((tm, 128), tab), pl.BlockSpec((tm, 128), tab)],
        out_specs=[pl.BlockSpec((tm, RW_PROJ), row), pl.BlockSpec((tm, MLA_HEADS * KCAT), row),
                   pl.BlockSpec((tm, KCAT), row), pl.BlockSpec((tm, KV_RANK), row), pl.BlockSpec((tm, QK_ROPE), row)],
        out_shape=[jax.ShapeDtypeStruct((n, RW_PROJ), F32), jax.ShapeDtypeStruct((n, MLA_HEADS * KCAT), BF16),
                   jax.ShapeDtypeStruct((n, KCAT), BF16), jax.ShapeDtypeStruct((n, KV_RANK), F32),
                   jax.ShapeDtypeStruct((n, QK_ROPE), F32)],
        compiler_params=_cparams(("parallel",), 48),
        name="mix_in",
    )(x2d, pw["norm_mix"], pw["w_r"], pw["w_m"], pw["q_norm"], pw["w_qb"], pw["kv_norm"], pw["wk"], c4, s4)
    return outs


def _rwkv_kernel(t_valid, proj_ref, shift_ref, st0_ref, mu_ref, vec_ref, w2a2_ref, g2_ref, ones_ref,
                 y_ref, st_ref, s_scr, carry_scr):
    C = RW_CHUNK
    tb = proj_ref.shape[1]

    @pl.when(pl.program_id(1) == 0)
    def _():
        s_scr[...] = st0_ref[0]
        carry_scr[...] = shift_ref[0]

    mu = mu_ref[...]
    w0 = vec_ref[0:1, :]
    a0 = vec_ref[1:2, :]
    k_k = vec_ref[2:3, :]
    k_a = vec_ref[3:4, :]
    r_k = vec_ref[4:5, :]
    lnx_w = vec_ref[5:6, :]
    lnx_b = vec_ref[6:7, :]
    ones = ones_ref[...]

    def bsum(x):
        hi, lo = _split2(x)
        return _nn(hi, ones) + _nn(lo, ones)

    row = lax.broadcasted_iota(I32, (C, 1), 0)
    lane128 = lax.broadcasted_iota(I32, (C, 128), 1)
    m0 = lane128 < RW_HEAD_DIM
    rr = lax.broadcasted_iota(I32, (2 * C, 2 * C), 0)
    cc = lax.broadcasted_iota(I32, (2 * C, 2 * C), 1)
    strict = cc < rr
    incl = cc <= rr
    eye = jnp.where(cc == rr, 1.0, 0.0).astype(F32)
    tr = lax.broadcasted_iota(I32, (C, C), 0)
    tc = lax.broadcasted_iota(I32, (C, C), 1)
    tri = jnp.where(tc <= tr, 1.0, 0.0).astype(BF16)

    def stack(x):
        return jnp.concatenate([jnp.where(m0, x, 0.0), jnp.where(m0, 0.0, x)], axis=0).astype(BF16)

    def chunk(c, carry):
        r0 = pl.multiple_of(c * C, C)
        x = proj_ref[0, pl.ds(r0, C), :]
        prev = jnp.where(row == 0, carry_scr[...], pltpu.roll(x, 1, axis=0))
        carry_scr[...] = x[C - 1:C, :]
        mixed = x + (prev - x) * mu
        r = mixed[:, 0:512]
        k = mixed[:, 512:1024]
        v = mixed[:, 1024:1536]
        wa = mixed[:, 1536:1664]
        gd = mixed[:, 1664:1792]
        zw = w0 + _nn(jnp.tanh(wa).astype(BF16), w2a2_ref[0])
        sp = jnp.maximum(-zw, 0.0) + jnp.log(1.0 + jnp.exp(-jnp.abs(zw)))
        logw = -jnp.exp(-sp - 0.5)
        a_lr = _sigmoid(a0 + _nn(wa.astype(BF16), w2a2_ref[1]))
        g = _nn(_sigmoid(gd).astype(BF16), g2_ref[...])
        kk = k * k_k
        kk = kk * lax.rsqrt(bsum(kk * kk) + L2_EPS)
        k_mod = k * (1.0 + (a_lr - 1.0) * k_a)
        a_v = -kk
        b_v = kk * a_lr
        if t_valid < C:
            valid = row < t_valid
            logw = jnp.where(valid, logw, 0.0)
            a_v = jnp.where(valid, a_v, 0.0)
            b_v = jnp.where(valid, b_v, 0.0)
            k_mod = jnp.where(valid, k_mod, 0.0)
            v = jnp.where(valid, v, 0.0)
        p1, p2, p3 = _split3(logw)
        cum = _nn(tri, p1) + _nn(tri, p2) + _nn(tri, p3)
        cum_last = cum[C - 1:C, :]
        e_neg = jnp.exp(-cum)
        e_rem = jnp.exp(cum_last - cum)
        r_t = r * jnp.exp(cum)
        a_t = a_v * jnp.exp(cum - logw)
        b_t = b_v * e_neg
        k_t = k_mod * e_neg
        b_g = b_v * e_rem
        k_g = k_mod * e_rem
        gam = jnp.exp(cum_last)

        ys = []
        for p in range(RW_HEADS // 2):
            sl = slice(128 * p, 128 * p + 128)
            a2, b2, k2, r2, v2 = stack(a_t[:, sl]), stack(b_t[:, sl]), stack(k_t[:, sl]), stack(r_t[:, sl]), stack(v[:, sl])
            bg2, kg2 = stack(b_g[:, sl]), stack(k_g[:, sl])
            gram = _nt(jnp.concatenate([a2, r2], axis=0), jnp.concatenate([b2, k2], axis=0))
            n_ab = jnp.where(strict, gram[:2 * C, :2 * C], 0.0)
            a_ak = jnp.where(strict, gram[:2 * C, 2 * C:], 0.0)
            a_rb = jnp.where(incl, gram[2 * C:, :2 * C], 0.0)
            a_rk = jnp.where(incl, gram[2 * C:, 2 * C:], 0.0)
            tinv = eye + n_ab
            nb = n_ab.astype(BF16)
            pw = _nn(nb, nb)
            steps = max(C.bit_length() - 1, 1)
            for it in range(1, steps):
                pb = pw.astype(BF16)
                if it < steps - 1:
                    res = _nn(pb, jnp.concatenate([tinv.astype(BF16), pb], axis=1))
                    tinv = tinv + res[:, :2 * C]
                    pw = res[:, 2 * C:]
                else:
                    tinv = tinv + _nn(pb, tinv.astype(BF16))
            av = _nn(jnp.concatenate([a_ak, a_rk], axis=0).astype(BF16), v2)
            tw = _nn(tinv.astype(BF16), jnp.concatenate([a2, av[:2 * C].astype(BF16)], axis=1))
            s_old = s_scr[p]
            sb = s_old.astype(BF16)
            u = _nt(tw[:, :2 * C].astype(BF16), sb) + tw[:, 2 * C:]
            ub = u.astype(BF16)
            y2 = _nt(r2, sb) + _nn(a_rb.astype(BF16), ub) + av[2 * C:]
            s_scr[p] = s_old * gam[:, sl] + _tn(jnp.concatenate([ub, v2], axis=0), jnp.concatenate([bg2, kg2], axis=0))
            ys.append(y2[:C] + y2[C:])
        y = jnp.concatenate(ys, axis=1)
        d = y - bsum(y) * (1.0 / RW_HEAD_DIM)
        var = bsum(d * d) * (1.0 / RW_HEAD_DIM)
        yn = d * lax.rsqrt(var + GN_EPS) * lnx_w + lnx_b
        bonus = bsum(r * k_mod * r_k) * v
        y_ref[0, pl.ds(r0, C), :] = ((yn + bonus) * g).astype(BF16)
        return carry

    lax.fori_loop(0, tb // C, chunk, 0)

    @pl.when(pl.program_id(1) == pl.num_programs(1) - 1)
    def _():
        st_ref[0] = s_scr[...]


def _pair_state(s):
    b = s.shape[0]
    s = s.reshape(b, 4, 2, 64, 64)
    z = jnp.zeros_like(s[:, :, 0])
    top = jnp.concatenate([s[:, :, 0], z], axis=-1)
    bot = jnp.concatenate([z, s[:, :, 1]], axis=-1)
    return jnp.concatenate([top, bot], axis=-2)


def _unpair_state(s2):
    b = s2.shape[0]
    return jnp.stack([s2[:, :, :64, :64], s2[:, :, 64:, 64:]], axis=2).reshape(b, RW_HEADS, 64, 64)


def _rwkv(proj_r3, shift_prev, state0, pw, t_valid, tb):
    b, t, _ = proj_r3.shape
    y, st = pl.pallas_call(
        functools.partial(_rwkv_kernel, t_valid),
        grid=(b, t // tb),
        in_specs=[pl.BlockSpec((1, tb, RW_PROJ), lambda i, j: (i, j, 0)),
                  pl.BlockSpec((1, 1, RW_PROJ), lambda i, j: (i, 0, 0)),
                  pl.BlockSpec((1, 4, 128, 128), lambda i, j: (i, 0, 0, 0)),
                  _full((1, RW_PROJ)), _full((8, RW_WIDTH)), _full((2, 128, RW_WIDTH)), _full((128, RW_WIDTH)),
                  _full((RW_WIDTH, RW_WIDTH))],
        out_specs=[pl.BlockSpec((1, tb, RW_WIDTH), lambda i, j: (i, j, 0)),
                   pl.BlockSpec((1, 4, 128, 128), lambda i, j: (i, 0, 0, 0))],
        out_shape=[jax.ShapeDtypeStruct((b, t, RW_WIDTH), BF16), jax.ShapeDtypeStruct((b, 4, 128, 128), F32)],
        scratch_shapes=[pltpu.VMEM((4, 128, 128), F32), pltpu.VMEM((1, RW_PROJ), F32)],
        compiler_params=_cparams(("parallel", "arbitrary"), 40),
        name="rwkv7",
    )(proj_r3, shift_prev.reshape(b, 1, RW_PROJ), _pair_state(state0), pw["mu"], pw["rw_vec"], pw["w2a2"], pw["g2"],
      pw["ones64"])
    return y, _unpair_state(st)


def _out_proj(o_heads, x, yrw, wv_ref, on_ref, wo_ref):
    ys = [_nn(o_heads[h].astype(BF16), wv_ref[h]) for h in range(MLA_HEADS)]
    y_mla = _rms(jnp.concatenate(ys, axis=1), on_ref[...]).astype(BF16)
    return x + _nn(yrw, wo_ref[0:RW_WIDTH, :]) + _nn(y_mla, wo_ref[RW_WIDTH:, :])


def _mla_prompt_kernel(tk, q_ref, k_ref, x_ref, yrw_ref, wv_ref, on_ref, wo_ref, o_ref, acc_scr, m_scr, l_scr):
    tq = q_ref.shape[0]
    i = pl.program_id(1)
    acc_scr[...] = jnp.zeros_like(acc_scr)
    m_scr[...] = jnp.full_like(m_scr, NEG_BIG)
    l_scr[...] = jnp.zeros_like(l_scr)
    qpos = i * tq + lax.broadcasted_iota(I32, (tq, tk), 0)
    kidx = lax.broadcasted_iota(I32, (tq, tk), 1)
    n_kv = (i * tq + tq + tk - 1) // tk

    def body(j, carry):
        k0 = pl.multiple_of(j * tk, tk)
        kc = k_ref[0, pl.ds(k0, tk), :]
        vc = kc[:, :KV_RANK]
        mask = (kidx + j * tk) <= qpos
        for h in range(MLA_HEADS):
            s = _nt(q_ref[:, KCAT * h:KCAT * (h + 1)], kc)
            s = jnp.where(mask, s, NEG_BIG)
            m_old = m_scr[h]
            m_new = jnp.maximum(m_old, jnp.max(s, axis=1, keepdims=True))
            p = jnp.exp(s - m_new)
            corr = jnp.exp(m_old - m_new)
            l_scr[h] = l_scr[h] * corr + jnp.sum(p, axis=1, keepdims=True)
            acc_scr[h] = acc_scr[h] * corr + _nn(p.astype(BF16), vc)
            m_scr[h] = m_new
        return carry

    lax.fori_loop(0, n_kv, body, 0)
    o_heads = [acc_scr[h] / l_scr[h] for h in range(MLA_HEADS)]
    o_ref[...] = _out_proj(o_heads, x_ref[...], yrw_ref[...], wv_ref, on_ref, wo_ref)


def _mla_prompt(qcat, kcat3, x2d, yrw2d, pw, tq, tk):
    b, t, _ = kcat3.shape
    nq = t // tq
    row = lambda i, j: (i * nq + j, 0)
    return pl.pallas_call(
        functools.partial(_mla_prompt_kernel, tk),
        grid=(b, nq),
        in_specs=[pl.BlockSpec((tq, MLA_HEADS * KCAT), row), pl.BlockSpec((1, t, KCAT), lambda i, j: (i, 0, 0)),
                  pl.BlockSpec((tq, D_MODEL), row), pl.BlockSpec((tq, RW_WIDTH), row),
                  _full((MLA_HEADS, KV_RANK, V_HEAD)), _full((1, 512)), _full((D_MODEL, D_MODEL))],
        out_specs=pl.BlockSpec((tq, D_MODEL), row),
        out_shape=jax.ShapeDtypeStruct((b * t, D_MODEL), F32),
        scratch_shapes=[pltpu.VMEM((MLA_HEADS, tq, KV_RANK), F32), pltpu.VMEM((MLA_HEADS, tq, 1), F32),
                        pltpu.VMEM((MLA_HEADS, tq, 1), F32)],
        compiler_params=_cparams(("parallel", "arbitrary"), 40),
        name="mla_prompt",
    )(qcat, kcat3, x2d, yrw2d, pw["wv"], pw["out_norm"], pw["w_out"])


def _out_proj_kernel(o_ref, x_ref, yrw_ref, wv_ref, on_ref, wo_ref, out_ref):
    o_heads = [o_ref[:, KV_RANK * h:KV_RANK * (h + 1)] for h in range(MLA_HEADS)]
    out_ref[...] = _out_proj(o_heads, x_ref[...], yrw_ref[...], wv_ref, on_ref, wo_ref)


def _out_proj_call(o_lat, x2d, yrw2d, pw, tm):
    n = x2d.shape[0]
    row = lambda i: (i, 0)
    return pl.pallas_call(
        _out_proj_kernel,
        grid=(n // tm,),
        in_specs=[pl.BlockSpec((tm, MLA_HEADS * KV_RANK), row), pl.BlockSpec((tm, D_MODEL), row),
                  pl.BlockSpec((tm, RW_WIDTH), row), _full((MLA_HEADS, KV_RANK, V_HEAD)), _full((1, 512)),
                  _full((D_MODEL, D_MODEL))],
        out_specs=pl.BlockSpec((tm, D_MODEL), row),
        out_shape=jax.ShapeDtypeStruct((n, D_MODEL), F32),
        compiler_params=_cparams(("parallel",), 32),
        name="mla_out_proj",
    )(o_lat, x2d, yrw2d, pw["wv"], pw["out_norm"], pw["w_out"])


def _paged_attn_kernel(pg, t_new, pt_ref, ql_ref, qr_ref, nck_ref, nkr_ref, *rest):
    ck_refs = rest[:pg]
    kr_refs = rest[pg:2 * pg]
    o_ref = rest[2 * pg]
    m_scr, l_scr, acc_scr = rest[2 * pg + 1:]
    g = pl.program_id(1)

    @pl.when(g == 0)
    def _():
        m_scr[...] = jnp.full_like(m_scr, NEG_BIG)
        l_scr[...] = jnp.zeros_like(l_scr)
        acc_scr[...] = jnp.zeros_like(acc_scr)

    ql = ql_ref[0]
    qr = qr_ref[0]

    def merge(s_list, v_list):
        s = jnp.concatenate(s_list, axis=1)
        m_old = m_scr[...]
        m_new = jnp.maximum(m_old, jnp.max(s, axis=1, keepdims=True))
        p = jnp.exp(s - m_new)
        corr = jnp.exp(m_old - m_new)
        l_scr[...] = l_scr[...] * corr + jnp.sum(p, axis=1, keepdims=True)
        pv = _nn(p[:, :PAGE_SIZE].astype(BF16), v_list[0])
        for j in range(1, len(v_list)):
            pv = pv + _nn(p[:, PAGE_SIZE * j:PAGE_SIZE * (j + 1)].astype(BF16), v_list[j])
        acc_scr[...] = acc_scr[...] * corr + pv
        m_scr[...] = m_new

    cks = [r[...].astype(BF16) for r in ck_refs]
    merge([_nt(ql, cks[j]) + _nt(qr, kr_refs[j][...].astype(BF16)) for j in range(pg)], cks)

    @pl.when(g == pl.num_programs(1) - 1)
    def _():
        nck = nck_ref[0].astype(BF16)
        s = _nt(ql, nck) + _nt(qr, nkr_ref[0].astype(BF16))
        rows = lax.broadcasted_iota(I32, s.shape, 0)
        cols = lax.broadcasted_iota(I32, s.shape, 1)
        s = jnp.where(cols <= rows % t_new, s, NEG_BIG)
        merge([s], [nck])
        o_ref[0] = acc_scr[...] / l_scr[...]


def _paged_attn(q_lat, q_rope, new_ckv, new_kr, cache_ckv, cache_krope, page_table, t_new, pg):
    b, n_pages = page_table.shape
    nq = q_lat.shape[1]
    ck_specs = [pl.BlockSpec((None, None, PAGE_SIZE, KV_RANK),
                             (lambda i, g, pt, j=j: (0, pt[i * n_pages + g * pg + j], 0, 0))) for j in range(pg)]
    kr_specs = [pl.BlockSpec((None, None, PAGE_SIZE, QK_ROPE),
                             (lambda i, g, pt, j=j: (0, pt[i * n_pages + g * pg + j], 0, 0))) for j in range(pg)]
    bmap = lambda i, g, pt: (i, 0, 0)
    gs = pltpu.PrefetchScalarGridSpec(
        num_scalar_prefetch=1,
        grid=(b, n_pages // pg),
        in_specs=[pl.BlockSpec((1, nq, KV_RANK), bmap), pl.BlockSpec((1, nq, QK_ROPE), bmap),
                  pl.BlockSpec((1, PAGE_SIZE, KV_RANK), bmap), pl.BlockSpec((1, PAGE_SIZE, QK_ROPE), bmap)]
        + ck_specs + kr_specs,
        out_specs=pl.BlockSpec((1, nq, KV_RANK), bmap),
        scratch_shapes=[pltpu.VMEM((nq, 1), F32), pltpu.VMEM((nq, 1), F32), pltpu.VMEM((nq, KV_RANK), F32)],
    )
    return pl.pallas_call(
        functools.partial(_paged_attn_kernel, pg, t_new),
        grid_spec=gs,
        out_shape=jax.ShapeDtypeStruct((b, nq, KV_RANK), F32),
        compiler_params=_cparams(("parallel", "arbitrary"), 32),
        name="mla_paged",
    )(page_table.reshape(-1), q_lat, q_rope, new_ckv, new_kr, *([cache_ckv] * pg), *([cache_krope] * pg))


def _mem_kv_kernel(m_ref, nm_ref, wk_ref, wv_ref, k_ref, v_ref):
    mn = _rms(m_ref[...], nm_ref[...]).astype(BF16)
    k_ref[...] = _nn(mn, wk_ref[...])
    v_ref[...] = _nn(mn, wv_ref[...])


def _mem_kv(mem2d, pw, tm):
    n = mem2d.shape[0]
    row = lambda i: (i, 0)
    return pl.pallas_call(
        _mem_kv_kernel,
        grid=(n // tm,),
        in_specs=[pl.BlockSpec((tm, D_MODEL), row), _full((1, D_MODEL)), _full((D_MODEL, X_WIDTH)),
                  _full((D_MODEL, X_WIDTH))],
        out_specs=[pl.BlockSpec((tm, X_WIDTH), row), pl.BlockSpec((tm, X_WIDTH), row)],
        out_shape=[jax.ShapeDtypeStruct((n, X_WIDTH), F32)] * 2,
        compiler_params=_cparams(("parallel",), 32),
        name="mem_kv",
    )(mem2d, pw["norm_mem"], pw["xa_wk"], pw["xa_wv"])


def _xattn_kernel(h_ref, mk_ref, mv_ref, nx_ref, wq_ref, wo_ref, o_ref):
    h = h_ref[...]
    xn = _rms(h, nx_ref[...]).astype(BF16)
    q = (_nn(xn, wq_ref[...]) * X_SCALE).astype(BF16)
    mk = mk_ref[0].astype(BF16)
    mv = mv_ref[0].astype(BF16)
    outs = []
    for hh in range(X_HEADS):
        sl = slice(X_HEAD_DIM * hh, X_HEAD_DIM * (hh + 1))
        s = _nt(q[:, sl], mk[:, sl])
        p = jnp.exp(s - jnp.max(s, axis=1, keepdims=True))
        outs.append(_nn(p.astype(BF16), mv[:, sl]) / jnp.sum(p, axis=1, keepdims=True))
    o_ref[...] = h + _nn(jnp.concatenate(outs, axis=1).astype(BF16), wo_ref[...])


def _xattn(h2d, mem_k, mem_v, seq_len, pw, tm):
    n = h2d.shape[0]
    nt = seq_len // tm
    row = lambda i, j: (i * nt + j, 0)
    bm = lambda i, j: (i, 0, 0)
    return pl.pallas_call(
        _xattn_kernel,
        grid=(n // seq_len, nt),
        in_specs=[pl.BlockSpec((tm, D_MODEL), row), pl.BlockSpec((1, MEM_TOKENS, X_WIDTH), bm),
                  pl.BlockSpec((1, MEM_TOKENS, X_WIDTH), bm), _full((1, D_MODEL)), _full((D_MODEL, X_WIDTH)),
                  _full((X_WIDTH, D_MODEL))],
        out_specs=pl.BlockSpec((tm, D_MODEL), row),
        out_shape=jax.ShapeDtypeStruct((n, D_MODEL), F32),
        compiler_params=_cparams(("parallel", "arbitrary"), 32),
        name="mem_xattn",
    )(h2d, mem_k, mem_v, pw["norm_x"], pw["xa_wq"], pw["xa_wo"])


def _router_kernel(h_ref, nf_ref, rw_ref, rb_ref, e_ref, g_ref, rk_ref, cnt_ref, cnt_scr):
    tm = h_ref.shape[0]

    @pl.when(pl.program_id(0) == 0)
    def _():
        cnt_scr[...] = jnp.zeros_like(cnt_scr)

    xn = _rms(h_ref[...], nf_ref[...])
    x_hi, x_lo = _split2(xn)
    w_hi, w_lo = _split2(rw_ref[...])
    logits = _nt(w_hi, x_hi) + _nt(w_hi, x_lo) + _nt(w_lo, x_hi) + rb_ref[...]
    eid = lax.broadcasted_iota(I32, logits.shape, 0)
    vals, hots = [], []
    l = logits
    for k in range(TOP_K):
        m = jnp.max(l, axis=0, keepdims=True)
        idx = jnp.min(jnp.where(l == m, eid, N_EXPERTS), axis=0, keepdims=True)
        hot = eid == idx
        e_ref[k:k + 1, :] = idx
        vals.append(m)
        hots.append(hot)
        l = jnp.where(hot, -jnp.inf, l)
    ex = [jnp.exp(vv - vals[0]) for vv in vals]
    den = ex[0] + ex[1] + ex[2] + ex[3]
    for k in range(TOP_K):
        g_ref[k:k + 1, :] = ex[k] / den
    hot_all = jnp.where(hots[0] | hots[1] | hots[2] | hots[3], 1.0, 0.0).astype(F32)
    ts = lax.broadcasted_iota(I32, (tm, tm), 0)
    tt = lax.broadcasted_iota(I32, (tm, tm), 1)
    upper = jnp.where(ts < tt, 1.0, 0.0).astype(BF16)
    before = _nn(hot_all.astype(BF16), upper) + cnt_scr[...]
    for k in range(TOP_K):
        rk_ref[k:k + 1, :] = jnp.sum(jnp.where(hots[k], before, 0.0), axis=0, keepdims=True).astype(I32)
    cnt_scr[...] = cnt_scr[...] + jnp.sum(hot_all, axis=1, keepdims=True)
    cnt_ref[...] = jnp.broadcast_to(cnt_scr[...], cnt_ref.shape).astype(I32)


def _router(h2d, pw, tm):
    n = h2d.shape[0]
    col = lambda i: (0, i)
    return pl.pallas_call(
        _router_kernel,
        grid=(n // tm,),
        in_specs=[pl.BlockSpec((tm, D_MODEL), lambda i: (i, 0)), _full((1, D_MODEL)), _full((N_EXPERTS, D_MODEL)),
                  _full((N_EXPERTS, 1))],
        out_specs=[pl.BlockSpec((TOP_K, tm), col), pl.BlockSpec((TOP_K, tm), col), pl.BlockSpec((TOP_K, tm), col),
                   _full((N_EXPERTS, 128))],
        out_shape=[jax.ShapeDtypeStruct((TOP_K, n), I32), jax.ShapeDtypeStruct((TOP_K, n), F32),
                   jax.ShapeDtypeStruct((TOP_K, n), I32), jax.ShapeDtypeStruct((N_EXPERTS, 128), I32)],
        scratch_shapes=[pltpu.VMEM((N_EXPERTS, 1), F32)],
        compiler_params=_cparams(("arbitrary",), 32),
        name="moe_router",
    )(h2d, pw["norm_ffn"], pw["router_wt"], pw["router_b"])


def _dispatch_kernel(rt, pe_ref, pd_ref, h_ref, nf_ref, dest_ref, xd_ref, xn_buf, zbuf, sem):
    tm = h_ref.shape[0]

    @pl.when(pl.program_id(0) == 0)
    def _():
        zbuf[...] = jnp.zeros_like(zbuf)
        for e in range(N_EXPERTS):
            @pl.when(pd_ref[e] > 0)
            def _():
                st = pl.multiple_of(pe_ref[e] - rt, rt)
                cp = pltpu.make_async_copy(zbuf, xd_ref.at[pl.ds(st, rt)], sem)
                cp.start()
                cp.wait()

    xn_buf[...] = _rms(h_ref[...], nf_ref[...])

    def issue(t, carry):
        for k in range(TOP_K):
            pltpu.make_async_copy(xn_buf.at[pl.ds(t, 1)], xd_ref.at[pl.ds(dest_ref[k, t], 1)], sem).start()
        return carry

    lax.fori_loop(0, tm, issue, 0)

    def drain(t, carry):
        for k in range(TOP_K):
            pltpu.make_async_copy(xn_buf.at[pl.ds(0, 1)], xd_ref.at[pl.ds(0, 1)], sem).wait()
        return carry

    lax.fori_loop(0, tm, drain, 0)


def _dispatch(h2d, dest3, pad_end, padded, cap, rt, pw, tm):
    n = h2d.shape[0]
    gs = pltpu.PrefetchScalarGridSpec(
        num_scalar_prefetch=2,
        grid=(n // tm,),
        in_specs=[pl.BlockSpec((tm, D_MODEL), lambda i, *_: (i, 0)), pl.BlockSpec((1, D_MODEL), lambda i, *_: (0, 0)),
                  pl.BlockSpec((None, TOP_K, tm), lambda i, *_: (i, 0, 0), memory_space=pltpu.SMEM)],
        out_specs=pl.BlockSpec(memory_space=pl.ANY),
        scratch_shapes=[pltpu.VMEM((tm, D_MODEL), F32), pltpu.VMEM((rt, D_MODEL), F32), pltpu.SemaphoreType.DMA],
    )
    return pl.pallas_call(
        functools.partial(_dispatch_kernel, rt),
        grid_spec=gs,
        out_shape=jax.ShapeDtypeStruct((cap, D_MODEL), F32),
        compiler_params=_cparams(("arbitrary",), 32),
        name="moe_dispatch",
    )(pad_end, padded, h2d, pw["norm_ffn"], dest3)


def _expert_kernel(te_ref, nu_ref, x_ref, wg_ref, bg_ref, wu_ref, bu_ref, wd_ref, bd_ref, y_ref):
    @pl.when(pl.program_id(0) < nu_ref[0])
    def _():
        x = x_ref[...].astype(BF16)
        gate = jnp.minimum(_nn(x, wg_ref[0]) + bg_ref[0], SWIGLU_LIMIT)
        up = jnp.clip(_nn(x, wu_ref[0]) + bu_ref[0], -SWIGLU_LIMIT, SWIGLU_LIMIT)
        hid = (up + 1.0) * (gate * _sigmoid(SWIGLU_ALPHA * gate))
        y_ref[...] = _nn(hid.astype(BF16), wd_ref[0]) + bd_ref[0]


def _experts(x_disp, tile_expert, n_used, pw, rt):
    cap = x_disp.shape[0]
    d_ff = pw["w_gate"].shape[2]
    tile = lambda i, te, nu: (jnp.minimum(i, nu[0] - 1), 0)
    wsel = lambda i, te, nu: (te[jnp.minimum(i, nu[0] - 1)], 0, 0)
    gs = pltpu.PrefetchScalarGridSpec(
        num_scalar_prefetch=2,
        grid=(cap // rt,),
        in_specs=[pl.BlockSpec((rt, D_MODEL), tile),
                  pl.BlockSpec((1, D_MODEL, d_ff), wsel), pl.BlockSpec((1, 1, d_ff), wsel),
                  pl.BlockSpec((1, D_MODEL, d_ff), wsel), pl.BlockSpec((1, 1, d_ff), wsel),
                  pl.BlockSpec((1, d_ff, D_MODEL), wsel), pl.BlockSpec((1, 1, D_MODEL), wsel)],
        out_specs=pl.BlockSpec((rt, D_MODEL), tile),
    )
    return pl.pallas_call(
        _expert_kernel,
        grid_spec=gs,
        out_shape=jax.ShapeDtypeStruct((cap, D_MODEL), F32),
        compiler_params=_cparams(("arbitrary",), 56),
        name="moe_experts",
    )(tile_expert, n_used, x_disp, pw["w_gate"], pw["b_gate"], pw["w_up"], pw["b_up"], pw["w_down"], pw["b_down"])


def _combine_kernel(h_ref, g_ref, nfin_ref, dest_ref, yd_ref, o_ref, buf, sem):
    tm = h_ref.shape[0]

    def issue(t, carry):
        for k in range(TOP_K):
            pltpu.make_async_copy(yd_ref.at[pl.ds(dest_ref[k, t], 1)], buf.at[k, pl.ds(t, 1)], sem).start()
        return carry

    lax.fori_loop(0, tm, issue, 0)

    def drain(t, carry):
        for k in range(TOP_K):
            pltpu.make_async_copy(yd_ref.at[pl.ds(0, 1)], buf.at[0, pl.ds(0, 1)], sem).wait()
        return carry

    lax.fori_loop(0, tm, drain, 0)
    g = g_ref[...]
    y = h_ref[...]
    for k in range(TOP_K):
        y = y + g[:, k:k + 1] * buf[k]
    o_ref[...] = _rms(y, nfin_ref[...])


def _combine(h2d, gates_t, dest3, y_disp, pw, tm):
    n = h2d.shape[0]
    row = lambda i: (i, 0)
    return pl.pallas_call(
        _combine_kernel,
        grid=(n // tm,),
        in_specs=[pl.BlockSpec((tm, D_MODEL), row), pl.BlockSpec((tm, TOP_K), row), _full((1, D_MODEL)),
                  pl.BlockSpec((None, TOP_K, tm), lambda i: (i, 0, 0), memory_space=pltpu.SMEM),
                  pl.BlockSpec(memory_space=pl.ANY)],
        out_specs=pl.BlockSpec((tm, D_MODEL), row),
        out_shape=jax.ShapeDtypeStruct((n, D_MODEL), F32),
        scratch_shapes=[pltpu.VMEM((TOP_K, tm, D_MODEL), F32), pltpu.SemaphoreType.DMA],
        compiler_params=_cparams(("arbitrary",), 32),
        name="moe_combine",
    )(h2d, gates_t, pw["norm_final"], dest3, y_disp)


def _moe_and_final_norm(h2d, pw, rt, tm_route, tm_move):
    n = h2d.shape[0]
    top_e, gates, rank, counts = _router(h2d, pw, tm_route)
    counts = counts[:, 0]
    padded = (counts + rt - 1) // rt * rt
    pad_end = jnp.cumsum(padded).astype(I32)
    pad_start = pad_end - padded
    n_tiles = (n * TOP_K) // rt + N_EXPERTS
    cap = n_tiles * rt
    dest = pad_start[top_e] + rank
    dest3 = dest.reshape(TOP_K, n // tm_move, tm_move).transpose(1, 0, 2)
    tile_expert = jnp.minimum(jnp.searchsorted(pad_end, jnp.arange(n_tiles, dtype=I32) * rt, side="right"),
                              N_EXPERTS - 1).astype(I32)
    n_used = (pad_end[-1:] // rt).astype(I32)
    x_disp = _dispatch(h2d, dest3, pad_end, padded.astype(I32), cap, rt, pw, tm_move)
    y_disp = _experts(x_disp, tile_expert, n_used, pw, rt)
    return _combine(h2d, gates.T, dest3, y_disp, pw, tm_move)


def _prep_weights(norm_mix, w_in, mu_shift, rw_w0, rw_w2, rw_a0, rw_a2, rw_g2, rw_k_k, rw_k_a, rw_r_k, rw_lnx_w,
                  rw_lnx_b, mla_q_norm, mla_w_qb, mla_kv_norm, mla_w_kvb, mla_out_norm, w_out, norm_x, norm_mem,
                  xa_wq, xa_wk, xa_wv, xa_wo, norm_ffn, router_w, router_b, moe_w_gate, moe_b_gate, moe_w_up,
                  moe_b_up, moe_w_down, moe_b_down, norm_final):
    w_m = w_in[:, RW_PROJ:]
    qb = mla_w_qb.reshape(Q_RANK, MLA_HEADS, QK_NOPE + QK_ROPE)
    w_kv = mla_w_kvb.reshape(KV_RANK, MLA_HEADS, QK_NOPE + V_HEAD)
    z64 = jnp.zeros((64, RW_WIDTH), F32)
    blk = jnp.arange(RW_WIDTH) // RW_HEAD_DIM
    return {
        "norm_mix": norm_mix.reshape(1, -1),
        "w_r": w_in[:, :RW_PROJ].astype(BF16),
        "w_m": jnp.concatenate([w_m[:, :640], jnp.tile(w_m[:, 640:672], (1, 4)), jnp.tile(w_m[:, 672:704], (1, 4))],
                               axis=1).astype(BF16),
        "q_norm": mla_q_norm.reshape(1, -1),
        "w_qb": jnp.concatenate([qb[:, :, :QK_NOPE].reshape(Q_RANK, -1), qb[:, :, QK_NOPE:QK_NOPE + 32].reshape(Q_RANK, -1),
                                 qb[:, :, QK_NOPE + 32:].reshape(Q_RANK, -1)], axis=1).astype(BF16),
        "kv_norm": mla_kv_norm.reshape(1, -1),
        "wk": jnp.transpose(w_kv[:, :, :QK_NOPE], (1, 2, 0)).astype(BF16),
        "wv": jnp.transpose(w_kv[:, :, QK_NOPE:], (1, 0, 2)).astype(BF16),
        "out_norm": mla_out_norm.reshape(1, -1),
        "w_out": w_out.astype(BF16),
        "mu": mu_shift.reshape(1, -1),
        "rw_vec": jnp.stack([rw_w0, rw_a0, rw_k_k, rw_k_a, rw_r_k.reshape(-1), rw_lnx_w, rw_lnx_b,
                             jnp.zeros_like(rw_w0)], axis=0),
        "w2a2": jnp.stack([jnp.concatenate([rw_w2, z64], axis=0), jnp.concatenate([z64, rw_a2], axis=0)]).astype(BF16),
        "g2": rw_g2.astype(BF16),
        "ones64": (blk[:, None] == blk[None, :]).astype(BF16),
        "norm_x": norm_x.reshape(1, -1),
        "norm_mem": norm_mem.reshape(1, -1),
        "xa_wq": xa_wq.astype(BF16), "xa_wk": xa_wk.astype(BF16), "xa_wv": xa_wv.astype(BF16),
        "xa_wo": xa_wo.astype(BF16),
        "norm_ffn": norm_ffn.reshape(1, -1),
        "router_wt": router_w.T,
        "router_b": router_b.reshape(-1, 1),
        "w_gate": moe_w_gate.astype(BF16), "b_gate": moe_b_gate[:, None, :],
        "w_up": moe_w_up.astype(BF16), "b_up": moe_b_up[:, None, :],
        "w_down": moe_w_down.astype(BF16), "b_down": moe_b_down[:, None, :],
        "norm_final": norm_final.reshape(1, -1),
    }


def _pick(n, prefs):
    for p in prefs:
        if n % p == 0:
            return p
    return n


def kernel(x_prompt, x_sample, mem_prompt, cache_ckv, cache_krope, cache_mem_k, cache_mem_v, state_rwkv, state_shift, page_table, norm_mix, w_in, mu_shift, rw_w0, rw_w2, rw_a0, rw_a2, rw_g2, rw_k_k, rw_k_a, rw_r_k, rw_lnx_w, rw_lnx_b, mla_q_norm, mla_w_qb, mla_kv_norm, mla_w_kvb, mla_out_norm, w_out, norm_x, norm_mem, xa_wq, xa_wk, xa_wv, xa_wo, norm_ffn, router_w, router_b, moe_w_gate, moe_b_gate, moe_w_up, moe_b_up, moe_w_down, moe_b_down, norm_final):
    assert w_in.shape[0] == 1, "single-layer trunk"
    bp, tp, _ = x_prompt.shape
    bs, ts, _ = x_sample.shape
    n_pages = page_table.shape[1]
    past_len = n_pages * PAGE_SIZE
    assert ts <= RW_CHUNK and tp % RW_CHUNK == 0
    pw = _prep_weights(norm_mix[0], w_in[0], mu_shift[0], rw_w0[0], rw_w2[0], rw_a0[0], rw_a2[0], rw_g2[0],
                       rw_k_k[0], rw_k_a[0], rw_r_k[0], rw_lnx_w[0], rw_lnx_b[0], mla_q_norm[0], mla_w_qb[0],
                       mla_kv_norm[0], mla_w_kvb[0], mla_out_norm[0], w_out[0], norm_x[0], norm_mem[0], xa_wq[0],
                       xa_wk[0], xa_wv[0], xa_wo[0], norm_ffn[0], router_w[0], router_b[0], moe_w_gate[0],
                       moe_b_gate[0], moe_w_up[0], moe_b_up[0], moe_w_down[0], moe_b_down[0], norm_final)
    np_, ns_ = bp * tp, bs * ts
    rt = _pick(ns_ * TOP_K, (512, 256, 128))

    xp2 = x_prompt.reshape(np_, D_MODEL)
    projr_p, qcat_p, kcat_p, ckv_p, krope_p = _mix_in(xp2, tp, 0, pw, _pick(tp, (256, 128, 64)))
    yrw_p, rwkv_p = _rwkv(projr_p.reshape(bp, tp, RW_PROJ), jnp.zeros((bp, RW_PROJ), F32),
                          jnp.zeros((bp, RW_HEADS, 64, 64), F32), pw, RW_CHUNK, _pick(tp, (256, 128, 64)))
    tq = _pick(tp, (256, 128, 64))
    h1_p = _mla_prompt(qcat_p, kcat_p.reshape(bp, tp, KCAT), xp2, yrw_p.reshape(np_, RW_WIDTH), pw, tq,
                       _pick(tp, (512, 256, 128, 64)))
    mk2, mv2 = _mem_kv(mem_prompt.reshape(bp * MEM_TOKENS, D_MODEL), pw, _pick(bp * MEM_TOKENS, (512, 256)))
    h2_p = _xattn(h1_p, mk2.reshape(bp, MEM_TOKENS, X_WIDTH), mv2.reshape(bp, MEM_TOKENS, X_WIDTH), tp, pw,
                  _pick(tp, (512, 256, 128, 64)))
    y_p = _moe_and_final_norm(h2_p, pw, rt, _pick(np_, (512, 256, 128)), _pick(np_, (256, 128)))

    xs2 = x_sample.reshape(ns_, D_MODEL)
    projr_s, qcat_s, _, ckv_s, krope_s = _mix_in(xs2, ts, past_len, pw, _pick(ns_, (256, 128, 64, 8)))
    projr_s3 = projr_s.reshape(bs, ts, RW_PROJ)
    projr_pad = jnp.pad(projr_s3, ((0, 0), (0, RW_CHUNK - ts), (0, 0)))
    yrw_s, rwkv_s = _rwkv(projr_pad, state_shift[0], state_rwkv[0], pw, ts, RW_CHUNK)
    yrw_s = yrw_s[:, :ts].reshape(ns_, RW_WIDTH)
    q4 = qcat_s.reshape(bs, ts, MLA_HEADS, KCAT).transpose(0, 2, 1, 3)
    q_lat = q4[..., :KV_RANK].reshape(bs, MLA_HEADS * ts, KV_RANK)
    o1 = q4[..., KV_RANK:KV_RANK + 128].reshape(bs, MLA_HEADS, ts, MLA_HEADS, 32)
    o2 = q4[..., KV_RANK + 128:].reshape(bs, MLA_HEADS, ts, MLA_HEADS, 32)
    q_rope = jnp.stack([jnp.concatenate([o1[:, h, :, h], o2[:, h, :, h]], axis=-1) for h in range(MLA_HEADS)],
                       axis=1).reshape(bs, MLA_HEADS * ts, QK_ROPE)
    new_ckv = jnp.pad(ckv_s.reshape(bs, ts, KV_RANK), ((0, 0), (0, PAGE_SIZE - ts), (0, 0)))
    new_kr = jnp.pad(krope_s.reshape(bs, ts, QK_ROPE), ((0, 0), (0, PAGE_SIZE - ts), (0, 0)))
    o_s = _paged_attn(q_lat, q_rope, new_ckv, new_kr, cache_ckv, cache_krope, page_table, ts,
                      _pick(n_pages, (16, 8, 4, 2, 1)))
    o_s = o_s.reshape(bs, MLA_HEADS, ts, KV_RANK).transpose(0, 2, 1, 3).reshape(ns_, MLA_HEADS * KV_RANK)
    h1_s = _out_proj_call(o_s, xs2, yrw_s, pw, _pick(ns_, (256, 128, 64, 8)))
    h2_s = _xattn(h1_s, cache_mem_k[0].reshape(bs, MEM_TOKENS, X_WIDTH), cache_mem_v[0].reshape(bs, MEM_TOKENS, X_WIDTH),
                  ts, pw, ts)
    y_s = _moe_and_final_norm(h2_s, pw, rt, _pick(ns_, (512, 256, 128)), _pick(ns_, (256, 128)))

    return (y_p.reshape(bp, tp, D_MODEL), y_s.reshape(bs, ts, D_MODEL),
            ckv_p.reshape(1, bp, tp, KV_RANK), krope_p.reshape(1, bp, tp, QK_ROPE),
            mk2.reshape(1, bp, MEM_TOKENS, X_HEADS, X_HEAD_DIM), mv2.reshape(1, bp, MEM_TOKENS, X_HEADS, X_HEAD_DIM),
            rwkv_p[None], projr_p.reshape(bp, tp, RW_PROJ)[None, :, -1],
            ckv_s.reshape(1, bs, ts, KV_RANK), krope_s.reshape(1, bs, ts, QK_ROPE),
            rwkv_s[None], projr_s3[None, :, -1])
```

```python
import functools

import jax
import jax.numpy as jnp
from jax import lax
from jax.experimental import pallas as pl
from jax.experimental.pallas import tpu as pltpu

F32 = jnp.float32
BF16 = jnp.bfloat16
I32 = jnp.int32

D_MODEL = 1024
PAGE_SIZE = 128
RW_HEADS = 8
RW_HEAD_DIM = 64
RW_WIDTH = 512
RW_PROJ = 1792
GN_EPS = 64e-5
L2_EPS = 1e-12
MLA_HEADS = 4
QK_NOPE = 128
QK_ROPE = 64
V_HEAD = 128
Q_RANK = 384
KV_RANK = 256
ROPE_THETA = 10000.0
MEM_TOKENS = 256
X_HEADS = 4
X_HEAD_DIM = 128
X_WIDTH = 512
N_EXPERTS = 32
TOP_K = 4
SWIGLU_LIMIT = 7.0
SWIGLU_ALPHA = 1.702
NORM_EPS = 1e-5
MLA_SCALE = (QK_NOPE + QK_ROPE) ** -0.5
X_SCALE = X_HEAD_DIM ** -0.5
NEG_BIG = -1e30

KCAT = KV_RANK + 2 * 128
RW_CHUNK = 64
VMEM_LIMIT_V7X = 56 * 1024 * 1024


def _cparams(sem, vmem_mib=None):
    kw = dict(dimension_semantics=sem)
    if vmem_mib is not None:
        kw["vmem_limit_bytes"] = min(vmem_mib * 1024 * 1024, VMEM_LIMIT_V7X)
    return pltpu.CompilerParams(**kw)


def _nn(a, b):
    return jnp.dot(a, b, preferred_element_type=F32)


def _nt(a, b):
    return lax.dot_general(a, b, (((1,), (1,)), ((), ())), preferred_element_type=F32)


def _tn(a, b):
    return lax.dot_general(a, b, (((0,), (0,)), ((), ())), preferred_element_type=F32)


def _split2(x):
    hi = x.astype(BF16)
    lo = (x - hi.astype(F32)).astype(BF16)
    return hi, lo


def _split3(x):
    p1 = x.astype(BF16)
    r1 = x - p1.astype(F32)
    p2 = r1.astype(BF16)
    p3 = (r1 - p2.astype(F32)).astype(BF16)
    return p1, p2, p3


def _rms(x, g, eps=NORM_EPS):
    return x * lax.rsqrt(jnp.mean(x * x, axis=-1, keepdims=True) + eps) * g


def _sigmoid(x):
    return 1.0 / (1.0 + jnp.exp(-x))


def _full(shape):
    n = len(shape)
    return pl.BlockSpec(shape, lambda *a: (0,) * n)


def _mix_in_kernel(x_ref, nm_ref, wr_ref, wm_ref, qn_ref, wqb_ref, kvn_ref, wk_ref, c4_ref, s4_ref,
                   projr_ref, qcat_ref, kcat_ref, ckv_ref, krope_ref):
    xn = _rms(x_ref[...], nm_ref[...]).astype(BF16)
    projr_ref[...] = _nn(xn, wr_ref[...])
    pm = _nn(xn, wm_ref[...])
    q_a = pm[:, :Q_RANK]
    lat = pm[:, Q_RANK:Q_RANK + KV_RANK]
    k1 = pm[:, 640:768]
    k2 = pm[:, 768:896]
    c4 = c4_ref[...]
    s4 = s4_ref[...]
    ckv = _rms(lat, kvn_ref[...])
    ckv_ref[...] = ckv
    ok1 = k1 * c4 - k2 * s4
    ok2 = k1 * s4 + k2 * c4
    krope_ref[...] = jnp.concatenate([ok1[:, :32], ok2[:, :32]], axis=1)
    kcat_ref[...] = jnp.concatenate([ckv, ok1, ok2], axis=1).astype(BF16)
    qn = _rms(q_a, qn_ref[...]).astype(BF16)
    q = _nn(qn, wqb_ref[...]) * MLA_SCALE
    r1 = q[:, 512:640]
    r2 = q[:, 640:768]
    o1 = r1 * c4 - r2 * s4
    o2 = r1 * s4 + r2 * c4
    lane = lax.broadcasted_iota(I32, o1.shape, 1)
    for h in range(MLA_HEADS):
        ql = _nn(q[:, 128 * h:128 * h + 128].astype(BF16), wk_ref[h])
        mh = (lane >= 32 * h) & (lane < 32 * h + 32)
        qcat_ref[:, KCAT * h:KCAT * (h + 1)] = jnp.concatenate(
            [ql, jnp.where(mh, o1, 0.0), jnp.where(mh, o2, 0.0)], axis=1).astype(BF16)


def _mix_in(x2d, seq_len, pos0, pw, tm):
    n = x2d.shape[0]
    half = QK_ROPE // 2
    inv = ROPE_THETA ** (-jnp.arange(half, dtype=F32) / half)
    pos = (pos0 + jnp.arange(seq_len, dtype=jnp.int32)).astype(F32)
    ang = pos[:, None] * inv[None, :]
    tab_len = max(seq_len, tm)
    c4 = jnp.tile(jnp.cos(ang), (tab_len // seq_len, 4))
    s4 = jnp.tile(jnp.sin(ang), (tab_len // seq_len, 4))
    ntab = tab_len // tm
    row = lambda i: (i, 0)
    tab = lambda i: (i % ntab, 0)
    outs = pl.pallas_call(
        _mix_in_kernel,
        grid=(n // tm,),
        in_specs=[pl.BlockSpec((tm, D_MODEL), row), _full((1, D_MODEL)), _full((D_MODEL, RW_PROJ)),
                  _full((D_MODEL, 896)), _full((1, Q_RANK)), _full((Q_RANK, 768)), _full((1, KV_RANK)),
                  _full((MLA_HEADS, QK_NOPE, KV_RANK)), pl.BlockSpec((tm, 128), tab), pl.BlockSpec((tm, 128), tab)],
        out_specs=[pl.BlockSpec((tm, RW_PROJ), row), pl.BlockSpec((tm, MLA_HEADS * KCAT), row),
                   pl.BlockSpec((tm, KCAT), row), pl.BlockSpec((tm, KV_RANK), row), pl.BlockSpec((tm, QK_ROPE), row)],
        out_shape=[jax.ShapeDtypeStruct((n, RW_PROJ), F32), jax.ShapeDtypeStruct((n, MLA_HEADS * KCAT), BF16),
                   jax.ShapeDtypeStruct((n, KCAT), BF16), jax.ShapeDtypeStruct((n, KV_RANK), F32),
                   jax.ShapeDtypeStruct((n, QK_ROPE), F32)],
        compiler_params=_cparams(("parallel",), 48),
        name="mix_in",
    )(x2d, pw["norm_mix"], pw["w_r"], pw["w_m"], pw["q_norm"], pw["w_qb"], pw["kv_norm"], pw["wk"], c4, s4)
    return outs


def _rwkv_kernel(t_valid, ng, proj_ref, shift_ref, st0_ref, mu_ref, vec_ref, w2a2_ref, g2_ref, ones_ref,
                 y_ref, st_ref, s_scr, carry_scr):
    C = RW_CHUNK
    lc = C.bit_length() - 1
    gr = ng * C
    tb = proj_ref.shape[1]
    npair = RW_HEADS // 2

    @pl.when(pl.program_id(1) == 0)
    def _():
        s_scr[...] = st0_ref[0]
        carry_scr[...] = shift_ref[0]

    mu = mu_ref[...]
    w0 = vec_ref[0:1, :]
    a0 = vec_ref[1:2, :]
    k_k = vec_ref[2:3, :]
    k_a = vec_ref[3:4, :]
    r_k = vec_ref[4:5, :]
    lnx_w = vec_ref[5:6, :]
    lnx_b = vec_ref[6:7, :]
    ones = ones_ref[...]

    def bsum(x):
        hi, lo = _split2(x)
        return _nn(hi, ones) + _nn(lo, ones)

    row = lax.broadcasted_iota(I32, (gr, 1), 0)
    lane128 = lax.broadcasted_iota(I32, (C, 128), 1)
    m0 = lane128 < RW_HEAD_DIM
    rr = lax.broadcasted_iota(I32, (2 * C, 2 * C), 0)
    cc = lax.broadcasted_iota(I32, (2 * C, 2 * C), 1)
    strict = cc < rr
    incl = cc <= rr
    eye = jnp.where(cc == rr, 1.0, 0.0).astype(F32)
    tr = lax.broadcasted_iota(I32, (gr, gr), 0)
    tc = lax.broadcasted_iota(I32, (gr, gr), 1)
    same = jnp.right_shift(tr, lc) == jnp.right_shift(tc, lc)
    tri = jnp.where(same & (tc <= tr), 1.0, 0.0).astype(BF16)
    allc = jnp.where(same, 1.0, 0.0).astype(BF16)

    def stack(x):
        return jnp.concatenate([jnp.where(m0, x, 0.0), jnp.where(m0, 0.0, x)], axis=0).astype(BF16)

    items = [(c, p) for c in range(ng) for p in range(npair)]

    def group(gi, carry):
        r0 = pl.multiple_of(gi * gr, gr)
        x = proj_ref[0, pl.ds(r0, gr), :]
        prev = jnp.where(row == 0, carry_scr[...], pltpu.roll(x, 1, axis=0))
        carry_scr[...] = x[gr - 1:gr, :]
        mixed = x + (prev - x) * mu
        r = mixed[:, 0:512]
        k = mixed[:, 512:1024]
        v = mixed[:, 1024:1536]
        wa = mixed[:, 1536:1664]
        gd = mixed[:, 1664:1792]
        zw = w0 + _nn(jnp.tanh(wa).astype(BF16), w2a2_ref[0])
        sp = jnp.maximum(-zw, 0.0) + jnp.log(1.0 + jnp.exp(-jnp.abs(zw)))
        logw = -jnp.exp(-sp - 0.5)
        a_lr = _sigmoid(a0 + _nn(wa.astype(BF16), w2a2_ref[1]))
        g = _nn(_sigmoid(gd).astype(BF16), g2_ref[...])
        kk = k * k_k
        kk = kk * lax.rsqrt(bsum(kk * kk) + L2_EPS)
        k_mod = k * (1.0 + (a_lr - 1.0) * k_a)
        a_v = -kk
        b_v = kk * a_lr
        if t_valid < C:
            valid = jnp.bitwise_and(row, C - 1) < t_valid
            logw = jnp.where(valid, logw, 0.0)
            a_v = jnp.where(valid, a_v, 0.0)
            b_v = jnp.where(valid, b_v, 0.0)
            k_mod = jnp.where(valid, k_mod, 0.0)
            v = jnp.where(valid, v, 0.0)
        p1, p2, p3 = _split3(logw)
        cum = _nn(tri, p1) + _nn(tri, p2) + _nn(tri, p3)
        tot = _nn(allc, p1) + _nn(allc, p2) + _nn(allc, p3)
        e_neg = jnp.exp(-cum)
        e_rem = jnp.exp(tot - cum)
        r_t = r * jnp.exp(cum)
        a_t = a_v * jnp.exp(cum - logw)
        b_t = b_v * e_neg
        k_t = k_mod * e_neg
        b_g = b_v * e_rem
        k_g = k_mod * e_rem
        gam = jnp.exp(tot)

        def blk(arr, c, p):
            return stack(arr[c * C:(c + 1) * C, 128 * p:128 * p + 128])

        a2 = [blk(a_t, c, p) for c, p in items]
        r2 = [blk(r_t, c, p) for c, p in items]
        v2 = [blk(v, c, p) for c, p in items]
        gram = [_nt(jnp.concatenate([a2[i], r2[i]], axis=0),
                    jnp.concatenate([blk(b_t, c, p), blk(k_t, c, p)], axis=0)) for i, (c, p) in enumerate(items)]
        tinv = [eye + jnp.where(strict, gm[:2 * C, :2 * C], 0.0) for gm in gram]
        nb = [jnp.where(strict, gm[:2 * C, :2 * C], 0.0).astype(BF16) for gm in gram]
        akrk = [jnp.concatenate([jnp.where(strict, gm[:2 * C, 2 * C:], 0.0), jnp.where(incl, gm[2 * C:, 2 * C:], 0.0)],
                                axis=0).astype(BF16) for gm in gram]
        arb = [jnp.where(incl, gm[2 * C:, :2 * C], 0.0).astype(BF16) for gm in gram]
        pw = [_nn(n, n) for n in nb]
        for it in range(1, lc):
            pb = [q.astype(BF16) for q in pw]
            if it < lc - 1:
                res = [_nn(pb[i], jnp.concatenate([tinv[i].astype(BF16), pb[i]], axis=1)) for i in range(len(items))]
                tinv = [tinv[i] + res[i][:, :2 * C] for i in range(len(items))]
                pw = [q[:, 2 * C:] for q in res]
            else:
                tinv = [tinv[i] + _nn(pb[i], tinv[i].astype(BF16)) for i in range(len(items))]
        av = [_nn(akrk[i], v2[i]) for i in range(len(items))]
        tw = [_nn(tinv[i].astype(BF16), jnp.concatenate([a2[i], av[i][:2 * C].astype(BF16)], axis=1))
              for i in range(len(items))]

        yrows = []
        for c in range(ng):
            idx = [c * npair + p for p in range(npair)]
            s_old = [s_scr[p] for p in range(npair)]
            sb = [s.astype(BF16) for s in s_old]
            u = [_nt(tw[i][:, :2 * C].astype(BF16), sb[p]) + tw[i][:, 2 * C:] for p, i in enumerate(idx)]
            ub = [q.astype(BF16) for q in u]
            y2 = [_nt(r2[i], sb[p]) + _nn(arb[i], ub[p]) + av[i][2 * C:] for p, i in enumerate(idx)]
            for p, i in enumerate(idx):
                s_scr[p] = s_old[p] * gam[c * C:c * C + 1, 128 * p:128 * p + 128] + _tn(
                    jnp.concatenate([ub[p], v2[i]], axis=0),
                    jnp.concatenate([blk(b_g, c, p), blk(k_g, c, p)], axis=0))
            yrows.append(jnp.concatenate([q[:C] + q[C:] for q in y2], axis=1))
        y = yrows[0] if ng == 1 else jnp.concatenate(yrows, axis=0)
        d = y - bsum(y) * (1.0 / RW_HEAD_DIM)
        var = bsum(d * d) * (1.0 / RW_HEAD_DIM)
        yn = d * lax.rsqrt(var + GN_EPS) * lnx_w + lnx_b
        bonus = bsum(r * k_mod * r_k) * v
        y_ref[0, pl.ds(r0, gr), :] = ((yn + bonus) * g).astype(BF16)
        return carry

    lax.fori_loop(0, tb // gr, group, 0)

    @pl.when(pl.program_id(1) == pl.num_programs(1) - 1)
    def _():
        st_ref[0] = s_scr[...]


def _pair_state(s):
    b = s.shape[0]
    s = s.reshape(b, 4, 2, 64, 64)
    z = jnp.zeros_like(s[:, :, 0])
    top = jnp.concatenate([s[:, :, 0], z], axis=-1)
    bot = jnp.concatenate([z, s[:, :, 1]], axis=-1)
    return jnp.concatenate([top, bot], axis=-2)


def _unpair_state(s2):
    b = s2.shape[0]
    return jnp.stack([s2[:, :, :64, :64], s2[:, :, 64:, 64:]], axis=2).reshape(b, RW_HEADS, 64, 64)


def _rwkv(proj_r3, shift_prev, state0, pw, t_valid, tb, ng):
    b, t, _ = proj_r3.shape
    y, st = pl.pallas_call(
        functools.partial(_rwkv_kernel, t_valid, ng),
        grid=(b, t // tb),
        in_specs=[pl.BlockSpec((1, tb, RW_PROJ), lambda i, j: (i, j, 0)),
                  pl.BlockSpec((1, 1, RW_PROJ), lambda i, j: (i, 0, 0)),
                  pl.BlockSpec((1, 4, 128, 128), lambda i, j: (i, 0, 0, 0)),
                  _full((1, RW_PROJ)), _full((8, RW_WIDTH)), _full((2, 128, RW_WIDTH)), _full((128, RW_WIDTH)),
                  _full((RW_WIDTH, RW_WIDTH))],
        out_specs=[pl.BlockSpec((1, tb, RW_WIDTH), lambda i, j: (i, j, 0)),
                   pl.BlockSpec((1, 4, 128, 128), lambda i, j: (i, 0, 0, 0))],
        out_shape=[jax.ShapeDtypeStruct((b, t, RW_WIDTH), BF16), jax.ShapeDtypeStruct((b, 4, 128, 128), F32)],
        scratch_shapes=[pltpu.VMEM((4, 128, 128), F32), pltpu.VMEM((1, RW_PROJ), F32)],
        compiler_params=_cparams(("parallel", "arbitrary"), 40),
        name="rwkv7",
    )(proj_r3, shift_prev.reshape(b, 1, RW_PROJ), _pair_state(state0), pw["mu"], pw["rw_vec"], pw["w2a2"], pw["g2"],
      pw["ones64"])
    return y, _unpair_state(st)


def _out_proj(o_heads, x, yrw, wv_ref, on_ref, wo_ref):
    ys = [_nn(o_heads[h].astype(BF16), wv_ref[h]) for h in range(MLA_HEADS)]
    y_mla = _rms(jnp.concatenate(ys, axis=1), on_ref[...]).astype(BF16)
    return x + _nn(yrw, wo_ref[0:RW_WIDTH, :]) + _nn(y_mla, wo_ref[RW_WIDTH:, :])


def _mla_prompt_kernel(tk, q_ref, k_ref, x_ref, yrw_ref, wv_ref, on_ref, wo_ref, o_ref, acc_scr, m_scr, l_scr):
    tq = q_ref.shape[0]
    i = pl.program_id(1)
    acc_scr[...] = jnp.zeros_like(acc_scr)
    m_scr[...] = jnp.full_like(m_scr, NEG_BIG)
    l_scr[...] = jnp.zeros_like(l_scr)
    qpos = i * tq + lax.broadcasted_iota(I32, (tq, tk), 0)
    kidx = lax.broadcasted_iota(I32, (tq, tk), 1)
    n_kv = (i * tq + tq + tk - 1) // tk

    def body(j, carry):
        k0 = pl.multiple_of(j * tk, tk)
        kc = k_ref[0, pl.ds(k0, tk), :]
        vc = kc[:, :KV_RANK]
        mask = (kidx + j * tk) <= qpos

        def qk(h):
            return _nt(q_ref[:, KCAT * h:KCAT * (h + 1)], kc)

        def soft_pv(h, s):
            s = jnp.where(mask, s, NEG_BIG)
            m_old = m_scr[h]
            m_new = jnp.maximum(m_old, jnp.max(s, axis=1, keepdims=True))
            p = jnp.exp(s - m_new)
            corr = jnp.exp(m_old - m_new)
            l_scr[h] = l_scr[h] * corr + jnp.sum(p, axis=1, keepdims=True)
            m_scr[h] = m_new
            acc_scr[h] = acc_scr[h] * corr + _nn(p.astype(BF16), vc)

        s_prev = qk(0)
        for h in range(1, MLA_HEADS):
            s_next = qk(h)
            soft_pv(h - 1, s_prev)
            s_prev = s_next
        soft_pv(MLA_HEADS - 1, s_prev)
        return carry

    lax.fori_loop(0, n_kv, body, 0)
    o_heads = [acc_scr[h] / l_scr[h] for h in range(MLA_HEADS)]
    o_ref[...] = _out_proj(o_heads, x_ref[...], yrw_ref[...], wv_ref, on_ref, wo_ref)


def _mla_prompt(qcat, kcat3, x2d, yrw2d, pw, tq, tk):
    b, t, _ = kcat3.shape
    nq = t // tq
    row = lambda i, j: (i * nq + j, 0)
    return pl.pallas_call(
        functools.partial(_mla_prompt_kernel, tk),
        grid=(b, nq),
        in_specs=[pl.BlockSpec((tq, MLA_HEADS * KCAT), row), pl.BlockSpec((1, t, KCAT), lambda i, j: (i, 0, 0)),
                  pl.BlockSpec((tq, D_MODEL), row), pl.BlockSpec((tq, RW_WIDTH), row),
                  _full((MLA_HEADS, KV_RANK, V_HEAD)), _full((1, 512)), _full((D_MODEL, D_MODEL))],
        out_specs=pl.BlockSpec((tq, D_MODEL), row),
        out_shape=jax.ShapeDtypeStruct((b * t, D_MODEL), F32),
        scratch_shapes=[pltpu.VMEM((MLA_HEADS, tq, KV_RANK), F32), pltpu.VMEM((MLA_HEADS, tq, 1), F32),
                        pltpu.VMEM((MLA_HEADS, tq, 1), F32)],
        compiler_params=_cparams(("parallel", "arbitrary"), 40),
        name="mla_prompt",
    )(qcat, kcat3, x2d, yrw2d, pw["wv"], pw["out_norm"], pw["w_out"])


def _out_proj_kernel(o_ref, x_ref, yrw_ref, wv_ref, on_ref, wo_ref, out_ref):
    o_heads = [o_ref[:, KV_RANK * h:KV_RANK * (h + 1)] for h in range(MLA_HEADS)]
    out_ref[...] = _out_proj(o_heads, x_ref[...], yrw_ref[...], wv_ref, on_ref, wo_ref)


def _out_proj_call(o_lat, x2d, yrw2d, pw, tm):
    n = x2d.shape[0]
    row = lambda i: (i, 0)
    return pl.pallas_call(
        _out_proj_kernel,
        grid=(n // tm,),
        in_specs=[pl.BlockSpec((tm, MLA_HEADS * KV_RANK), row), pl.BlockSpec((tm, D_MODEL), row),
                  pl.BlockSpec((tm, RW_WIDTH), row), _full((MLA_HEADS, KV_RANK, V_HEAD)), _full((1, 512)),
                  _full((D_MODEL, D_MODEL))],
        out_specs=pl.BlockSpec((tm, D_MODEL), row),
        out_shape=jax.ShapeDtypeStruct((n, D_MODEL), F32),
        compiler_params=_cparams(("parallel",), 32),
        name="mla_out_proj",
    )(o_lat, x2d, yrw2d, pw["wv"], pw["out_norm"], pw["w_out"])


def _paged_attn_kernel(pg, t_new, pt_ref, ql_ref, qr_ref, nck_ref, nkr_ref, *rest):
    ck_refs = rest[:pg]
    kr_refs = rest[pg:2 * pg]
    o_ref = rest[2 * pg]
    m_scr, l_scr, acc_scr, kall, krall = rest[2 * pg + 1:]
    g = pl.program_id(1)

    @pl.when(g == 0)
    def _():
        m_scr[...] = jnp.full_like(m_scr, NEG_BIG)
        l_scr[...] = jnp.zeros_like(l_scr)
        acc_scr[...] = jnp.zeros_like(acc_scr)

    ql = ql_ref[0]
    qr = qr_ref[0]

    def merge(s, v):
        m_old = m_scr[...]
        m_new = jnp.maximum(m_old, jnp.max(s, axis=1, keepdims=True))
        p = jnp.exp(s - m_new)
        corr = jnp.exp(m_old - m_new)
        l_scr[...] = l_scr[...] * corr + jnp.sum(p, axis=1, keepdims=True)
        acc_scr[...] = acc_scr[...] * corr + _nn(p.astype(BF16), v)
        m_scr[...] = m_new

    for j in range(pg):
        kall[PAGE_SIZE * j:PAGE_SIZE * (j + 1), :] = ck_refs[j][...].astype(BF16)
        krall[:, PAGE_SIZE * j:PAGE_SIZE * (j + 1)] = kr_refs[j][...].astype(BF16)
    ka = kall[...]
    merge(_nt(ql, ka) + _nn(qr, krall[...]), ka)

    @pl.when(g == pl.num_programs(1) - 1)
    def _():
        nck = nck_ref[0].astype(BF16)
        s = _nt(ql, nck) + _nt(qr, nkr_ref[0].astype(BF16))
        rows = lax.broadcasted_iota(I32, s.shape, 0)
        cols = lax.broadcasted_iota(I32, s.shape, 1)
        s = jnp.where(cols <= rows % t_new, s, NEG_BIG)
        merge(s, nck)
        o_ref[0] = acc_scr[...] / l_scr[...]


def _paged_attn(q_lat, q_rope, new_ckv, new_kr, cache_ckv, cache_krope_t, page_table, t_new, pg):
    b, n_pages = page_table.shape
    nq = q_lat.shape[1]
    ck_specs = [pl.BlockSpec((None, None, PAGE_SIZE, KV_RANK),
                             (lambda i, g, pt, j=j: (0, pt[i * n_pages + g * pg + j], 0, 0))) for j in range(pg)]
    kr_specs = [pl.BlockSpec((None, None, QK_ROPE, PAGE_SIZE),
                             (lambda i, g, pt, j=j: (0, pt[i * n_pages + g * pg + j], 0, 0))) for j in range(pg)]
    bmap = lambda i, g, pt: (i, 0, 0)
    gs = pltpu.PrefetchScalarGridSpec(
        num_scalar_prefetch=1,
        grid=(b, n_pages // pg),
        in_specs=[pl.BlockSpec((1, nq, KV_RANK), bmap), pl.BlockSpec((1, nq, QK_ROPE), bmap),
                  pl.BlockSpec((1, PAGE_SIZE, KV_RANK), bmap), pl.BlockSpec((1, PAGE_SIZE, QK_ROPE), bmap)]
        + ck_specs + kr_specs,
        out_specs=pl.BlockSpec((1, nq, KV_RANK), bmap),
        scratch_shapes=[pltpu.VMEM((nq, 1), F32), pltpu.VMEM((nq, 1), F32), pltpu.VMEM((nq, KV_RANK), F32),
                        pltpu.VMEM((pg * PAGE_SIZE, KV_RANK), BF16), pltpu.VMEM((QK_ROPE, pg * PAGE_SIZE), BF16)],
    )
    return pl.pallas_call(
        functools.partial(_paged_attn_kernel, pg, t_new),
        grid_spec=gs,
        out_shape=jax.ShapeDtypeStruct((b, nq, KV_RANK), F32),
        compiler_params=_cparams(("parallel", "arbitrary"), 32),
        name="mla_paged",
    )(page_table.reshape(-1), q_lat, q_rope, new_ckv, new_kr, *([cache_ckv] * pg), *([cache_krope_t] * pg))


def _mem_kv_kernel(m_ref, nm_ref, wk_ref, wv_ref, k_ref, v_ref):
    mn = _rms(m_ref[...], nm_ref[...]).astype(BF16)
    k_ref[...] = _nn(mn, wk_ref[...])
    v_ref[...] = _nn(mn, wv_ref[...])


def _mem_kv(mem2d, pw, tm):
    n = mem2d.shape[0]
    row = lambda i: (i, 0)
    return pl.pallas_call(
        _mem_kv_kernel,
        grid=(n // tm,),
        in_specs=[pl.BlockSpec((tm, D_MODEL), row), _full((1, D_MODEL)), _full((D_MODEL, X_WIDTH)),
                  _full((D_MODEL, X_WIDTH))],
        out_specs=[pl.BlockSpec((tm, X_WIDTH), row), pl.BlockSpec((tm, X_WIDTH), row)],
        out_shape=[jax.ShapeDtypeStruct((n, X_WIDTH), F32)] * 2,
        compiler_params=_cparams(("parallel",), 32),
        name="mem_kv",
    )(mem2d, pw["norm_mem"], pw["xa_wk"], pw["xa_wv"])


def _xattn_kernel(h_ref, mk_ref, mv_ref, nx_ref, wq_ref, wo_ref, o_ref):
    h = h_ref[...]
    xn = _rms(h, nx_ref[...]).astype(BF16)
    q = (_nn(xn, wq_ref[...]) * X_SCALE).astype(BF16)
    mk = mk_ref[0].astype(BF16)
    mv = mv_ref[0].astype(BF16)
    outs = []
    for hh in range(X_HEADS):
        sl = slice(X_HEAD_DIM * hh, X_HEAD_DIM * (hh + 1))
        s = _nt(q[:, sl], mk[:, sl])
        p = jnp.exp(s - jnp.max(s, axis=1, keepdims=True))
        outs.append(_nn(p.astype(BF16), mv[:, sl]) / jnp.sum(p, axis=1, keepdims=True))
    o_ref[...] = h + _nn(jnp.concatenate(outs, axis=1).astype(BF16), wo_ref[...])


def _xattn(h2d, mem_k, mem_v, seq_len, pw, tm):
    n = h2d.shape[0]
    nt = seq_len // tm
    row = lambda i, j: (i * nt + j, 0)
    bm = lambda i, j: (i, 0, 0)
    return pl.pallas_call(
        _xattn_kernel,
        grid=(n // seq_len, nt),
        in_specs=[pl.BlockSpec((tm, D_MODEL), row), pl.BlockSpec((1, MEM_TOKENS, X_WIDTH), bm),
                  pl.BlockSpec((1, MEM_TOKENS, X_WIDTH), bm), _full((1, D_MODEL)), _full((D_MODEL, X_WIDTH)),
                  _full((X_WIDTH, D_MODEL))],
        out_specs=pl.BlockSpec((tm, D_MODEL), row),
        out_shape=jax.ShapeDtypeStruct((n, D_MODEL), F32),
        compiler_params=_cparams(("parallel", "arbitrary"), 32),
        name="mem_xattn",
    )(h2d, mem_k, mem_v, pw["norm_x"], pw["xa_wq"], pw["xa_wo"])


def _router_kernel(h_ref, nf_ref, rw_ref, rb_ref, e_ref, g_ref, rk_ref, cnt_ref, cnt_scr):
    tm = h_ref.shape[0]

    @pl.when(pl.program_id(0) == 0)
    def _():
        cnt_scr[...] = jnp.zeros_like(cnt_scr)

    xn = _rms(h_ref[...], nf_ref[...])
    x_hi, x_lo = _split2(xn)
    w_hi, w_lo = _split2(rw_ref[...])
    logits = _nt(w_hi, x_hi) + _nt(w_hi, x_lo) + _nt(w_lo, x_hi) + rb_ref[...]
    eid = lax.broadcasted_iota(I32, logits.shape, 0)
    vals, hots = [], []
    l = logits
    for k in range(TOP_K):
        m = jnp.max(l, axis=0, keepdims=True)
        idx = jnp.min(jnp.where(l == m, eid, N_EXPERTS), axis=0, keepdims=True)
        hot = eid == idx
        e_ref[k:k + 1, :] = idx
        vals.append(m)
        hots.append(hot)
        l = jnp.where(hot, -jnp.inf, l)
    ex = [jnp.exp(vv - vals[0]) for vv in vals]
    den = ex[0] + ex[1] + ex[2] + ex[3]
    for k in range(TOP_K):
        g_ref[k:k + 1, :] = ex[k] / den
    hot_all = jnp.where(hots[0] | hots[1] | hots[2] | hots[3], 1.0, 0.0).astype(F32)
    ts = lax.broadcasted_iota(I32, (tm, tm), 0)
    tt = lax.broadcasted_iota(I32, (tm, tm), 1)
    upper = jnp.where(ts < tt, 1.0, 0.0).astype(BF16)
    before = _nn(hot_all.astype(BF16), upper) + cnt_scr[...]
    for k in range(TOP_K):
        rk_ref[k:k + 1, :] = jnp.sum(jnp.where(hots[k], before, 0.0), axis=0, keepdims=True).astype(I32)
    cnt_scr[...] = cnt_scr[...] + jnp.sum(hot_all, axis=1, keepdims=True)
    cnt_ref[...] = jnp.broadcast_to(cnt_scr[...], cnt_ref.shape).astype(I32)


def _router(h2d, pw, tm):
    n = h2d.shape[0]
    col = lambda i: (0, i)
    return pl.pallas_call(
        _router_kernel,
        grid=(n // tm,),
        in_specs=[pl.BlockSpec((tm, D_MODEL), lambda i: (i, 0)), _full((1, D_MODEL)), _full((N_EXPERTS, D_MODEL)),
                  _full((N_EXPERTS, 1))],
        out_specs=[pl.BlockSpec((TOP_K, tm), col), pl.BlockSpec((TOP_K, tm), col), pl.BlockSpec((TOP_K, tm), col),
                   _full((N_EXPERTS, 128))],
        out_shape=[jax.ShapeDtypeStruct((TOP_K, n), I32), jax.ShapeDtypeStruct((TOP_K, n), F32),
                   jax.ShapeDtypeStruct((TOP_K, n), I32), jax.ShapeDtypeStruct((N_EXPERTS, 128), I32)],
        scratch_shapes=[pltpu.VMEM((N_EXPERTS, 1), F32)],
        compiler_params=_cparams(("arbitrary",), 32),
        name="moe_router",
    )(h2d, pw["norm_ffn"], pw["router_wt"], pw["router_b"])


def _dispatch_kernel(rt, pe_ref, pd_ref, h_ref, nf_ref, dest_ref, xd_ref, xn_buf, zbuf, sems, zsem):
    tm = h_ref.shape[0]
    i = pl.program_id(0)
    slot = lax.rem(i, 2)

    @pl.when(i == 0)
    def _():
        zbuf[...] = jnp.zeros_like(zbuf)
        for e in range(N_EXPERTS):
            @pl.when(pd_ref[e] > 0)
            def _():
                st = pl.multiple_of(pe_ref[e] - rt, rt)
                cp = pltpu.make_async_copy(zbuf, xd_ref.at[pl.ds(st, rt)], zsem)
                cp.start()
                cp.wait()

    xn_buf[slot] = _rms(h_ref[...], nf_ref[...])

    def issue(t, carry):
        for k in range(TOP_K):
            pltpu.make_async_copy(xn_buf.at[slot, pl.ds(t, 1)], xd_ref.at[pl.ds(dest_ref[k, t], 1)],
                                  sems.at[slot]).start(priority=k % 2)
        return carry

    lax.fori_loop(0, tm, issue, 0, unroll=8)

    def drain(s):
        def body(t, carry):
            for k in range(TOP_K):
                pltpu.make_async_copy(xn_buf.at[0, pl.ds(0, 1)], xd_ref.at[pl.ds(0, 1)], sems.at[s]).wait()
            return carry
        lax.fori_loop(0, tm, body, 0, unroll=8)

    @pl.when(i > 0)
    def _():
        drain(1 - slot)

    @pl.when(i == pl.num_programs(0) - 1)
    def _():
        drain(slot)


def _dispatch(h2d, dest3, pad_end, padded, cap, rt, pw, tm):
    n = h2d.shape[0]
    gs = pltpu.PrefetchScalarGridSpec(
        num_scalar_prefetch=2,
        grid=(n // tm,),
        in_specs=[pl.BlockSpec((tm, D_MODEL), lambda i, *_: (i, 0)), pl.BlockSpec((1, D_MODEL), lambda i, *_: (0, 0)),
                  pl.BlockSpec((None, TOP_K, tm), lambda i, *_: (i, 0, 0), memory_space=pltpu.SMEM)],
        out_specs=pl.BlockSpec(memory_space=pl.ANY),
        scratch_shapes=[pltpu.VMEM((2, tm, D_MODEL), F32), pltpu.VMEM((rt, D_MODEL), F32),
                        pltpu.SemaphoreType.DMA((2,)), pltpu.SemaphoreType.DMA],
    )
    return pl.pallas_call(
        functools.partial(_dispatch_kernel, rt),
        grid_spec=gs,
        out_shape=jax.ShapeDtypeStruct((cap, D_MODEL), F32),
        compiler_params=_cparams(("arbitrary",), 32),
        name="moe_dispatch",
    )(pad_end, padded, h2d, pw["norm_ffn"], dest3)


def _expert_kernel(te_ref, nu_ref, x_ref, wg_ref, bg_ref, wu_ref, bu_ref, wd_ref, bd_ref, y_ref):
    @pl.when(pl.program_id(0) < nu_ref[0])
    def _():
        x = x_ref[...].astype(BF16)
        gate = jnp.minimum(_nn(x, wg_ref[0]) + bg_ref[0], SWIGLU_LIMIT)
        up = jnp.clip(_nn(x, wu_ref[0]) + bu_ref[0], -SWIGLU_LIMIT, SWIGLU_LIMIT)
        hid = (up + 1.0) * (gate * _sigmoid(SWIGLU_ALPHA * gate))
        y_ref[...] = _nn(hid.astype(BF16), wd_ref[0]) + bd_ref[0]


def _experts(x_disp, tile_expert, n_used, pw, rt):
    cap = x_disp.shape[0]
    d_ff = pw["w_gate"].shape[2]
    tile = lambda i, te, nu: (jnp.minimum(i, nu[0] - 1), 0)
    wsel = lambda i, te, nu: (te[jnp.minimum(i, nu[0] - 1)], 0, 0)
    gs = pltpu.PrefetchScalarGridSpec(
        num_scalar_prefetch=2,
        grid=(cap // rt,),
        in_specs=[pl.BlockSpec((rt, D_MODEL), tile),
                  pl.BlockSpec((1, D_MODEL, d_ff), wsel), pl.BlockSpec((1, 1, d_ff), wsel),
                  pl.BlockSpec((1, D_MODEL, d_ff), wsel), pl.BlockSpec((1, 1, d_ff), wsel),
                  pl.BlockSpec((1, d_ff, D_MODEL), wsel), pl.BlockSpec((1, 1, D_MODEL), wsel)],
        out_specs=pl.BlockSpec((rt, D_MODEL), tile),
    )
    return pl.pallas_call(
        _expert_kernel,
        grid_spec=gs,
        out_shape=jax.ShapeDtypeStruct((cap, D_MODEL), F32),
        compiler_params=_cparams(("arbitrary",), 56),
        name="moe_experts",
    )(tile_expert, n_used, x_disp, pw["w_gate"], pw["b_gate"], pw["w_up"], pw["b_up"], pw["w_down"], pw["b_down"])


def _combine_kernel(h_ref, g_ref, nfin_ref, dest_ref, destn_ref, yd_ref, o_ref, buf, sems):
    tm = h_ref.shape[0]
    i = pl.program_id(0)
    slot = lax.rem(i, 2)

    def issue(dref, s):
        def body(t, carry):
            for k in range(TOP_K):
                pltpu.make_async_copy(yd_ref.at[pl.ds(dref[k, t], 1)], buf.at[s, k, pl.ds(t, 1)],
                                      sems.at[s]).start(priority=k % 2)
            return carry
        lax.fori_loop(0, tm, body, 0, unroll=8)

    @pl.when(i == 0)
    def _():
        issue(dest_ref, 0)

    @pl.when(i + 1 < pl.num_programs(0))
    def _():
        issue(destn_ref, 1 - slot)

    def drain(t, carry):
        for k in range(TOP_K):
            pltpu.make_async_copy(yd_ref.at[pl.ds(0, 1)], buf.at[0, 0, pl.ds(0, 1)], sems.at[slot]).wait()
        return carry

    lax.fori_loop(0, tm, drain, 0, unroll=8)
    g = g_ref[...]
    y = h_ref[...]
    for k in range(TOP_K):
        y = y + g[:, k:k + 1] * buf[slot, k]
    o_ref[...] = _rms(y, nfin_ref[...])


def _combine(h2d, gates_t, dest3, y_disp, pw, tm):
    n = h2d.shape[0]
    nt = n // tm
    row = lambda i: (i, 0)
    return pl.pallas_call(
        _combine_kernel,
        grid=(nt,),
        in_specs=[pl.BlockSpec((tm, D_MODEL), row), pl.BlockSpec((tm, TOP_K), row), _full((1, D_MODEL)),
                  pl.BlockSpec((None, TOP_K, tm), lambda i: (i, 0, 0), memory_space=pltpu.SMEM),
                  pl.BlockSpec((None, TOP_K, tm), lambda i: (jnp.minimum(i + 1, nt - 1), 0, 0), memory_space=pltpu.SMEM),
                  pl.BlockSpec(memory_space=pl.ANY)],
        out_specs=pl.BlockSpec((tm, D_MODEL), row),
        out_shape=jax.ShapeDtypeStruct((n, D_MODEL), F32),
        scratch_shapes=[pltpu.VMEM((2, TOP_K, tm, D_MODEL), F32), pltpu.SemaphoreType.DMA((2,))],
        compiler_params=_cparams(("arbitrary",), 40),
        name="moe_combine",
    )(h2d, gates_t, pw["norm_final"], dest3, dest3, y_disp)


def _moe_and_final_norm(h2d, pw, rt, tm_route, tm_move):
    n = h2d.shape[0]
    top_e, gates, rank, counts = _router(h2d, pw, tm_route)
    counts = counts[:, 0]
    padded = (counts + rt - 1) // rt * rt
    pad_end = jnp.cumsum(padded).astype(I32)
    pad_start = pad_end - padded
    n_tiles = (n * TOP_K) // rt + N_EXPERTS
    cap = n_tiles * rt
    eids = jnp.arange(N_EXPERTS, dtype=I32)[:, None, None]
    dest = rank + jnp.sum(jnp.where(top_e[None] == eids, pad_start[:, None, None], 0), axis=0)
    dest3 = dest.reshape(TOP_K, n // tm_move, tm_move).transpose(1, 0, 2)
    tile_expert = jnp.minimum(jnp.sum((pad_end[None, :] <= (jnp.arange(n_tiles, dtype=I32) * rt)[:, None]).astype(I32),
                                      axis=1), N_EXPERTS - 1).astype(I32)
    n_used = (pad_end[-1:] // rt).astype(I32)
    x_disp = _dispatch(h2d, dest3, pad_end, padded.astype(I32), cap, rt, pw, tm_move)
    y_disp = _experts(x_disp, tile_expert, n_used, pw, rt)
    return _combine(h2d, gates.T, dest3, y_disp, pw, tm_move)


def _prep_weights(norm_mix, w_in, mu_shift, rw_w0, rw_w2, rw_a0, rw_a2, rw_g2, rw_k_k, rw_k_a, rw_r_k, rw_lnx_w,
                  rw_lnx_b, mla_q_norm, mla_w_qb, mla_kv_norm, mla_w_kvb, mla_out_norm, w_out, norm_x, norm_mem,
                  xa_wq, xa_wk, xa_wv, xa_wo, norm_ffn, router_w, router_b, moe_w_gate, moe_b_gate, moe_w_up,
                  moe_b_up, moe_w_down, moe_b_down, norm_final):
    w_m = w_in[:, RW_PROJ:]
    qb = mla_w_qb.reshape(Q_RANK, MLA_HEADS, QK_NOPE + QK_ROPE)
    w_kv = mla_w_kvb.reshape(KV_RANK, MLA_HEADS, QK_NOPE + V_HEAD)
    z64 = jnp.zeros((64, RW_WIDTH), F32)
    blk = jnp.arange(RW_WIDTH) // RW_HEAD_DIM
    return {
        "norm_mix": norm_mix.reshape(1, -1),
        "w_r": w_in[:, :RW_PROJ].astype(BF16),
        "w_m": jnp.concatenate([w_m[:, :640], jnp.tile(w_m[:, 640:672], (1, 4)), jnp.tile(w_m[:, 672:704], (1, 4))],
                               axis=1).astype(BF16),
        "q_norm": mla_q_norm.reshape(1, -1),
        "w_qb": jnp.concatenate([qb[:, :, :QK_NOPE].reshape(Q_RANK, -1), qb[:, :, QK_NOPE:QK_NOPE + 32].reshape(Q_RANK, -1),
                                 qb[:, :, QK_NOPE + 32:].reshape(Q_RANK, -1)], axis=1).astype(BF16),
        "kv_norm": mla_kv_norm.reshape(1, -1),
        "wk": jnp.transpose(w_kv[:, :, :QK_NOPE], (1, 2, 0)).astype(BF16),
        "wv": jnp.transpose(w_kv[:, :, QK_NOPE:], (1, 0, 2)).astype(BF16),
        "out_norm": mla_out_norm.reshape(1, -1),
        "w_out": w_out.astype(BF16),
        "mu": mu_shift.reshape(1, -1),
        "rw_vec": jnp.stack([rw_w0, rw_a0, rw_k_k, rw_k_a, rw_r_k.reshape(-1), rw_lnx_w, rw_lnx_b,
                             jnp.zeros_like(rw_w0)], axis=0),
        "w2a2": jnp.stack([jnp.concatenate([rw_w2, z64], axis=0), jnp.concatenate([z64, rw_a2], axis=0)]).astype(BF16),
        "g2": rw_g2.astype(BF16),
        "ones64": (blk[:, None] == blk[None, :]).astype(BF16),
        "norm_x": norm_x.reshape(1, -1),
        "norm_mem": norm_mem.reshape(1, -1),
        "xa_wq": xa_wq.astype(BF16), "xa_wk": xa_wk.astype(BF16), "xa_wv": xa_wv.astype(BF16),
        "xa_wo": xa_wo.astype(BF16),
        "norm_ffn": norm_ffn.reshape(1, -1),
        "router_wt": router_w.T,
        "router_b": router_b.reshape(-1, 1),
        "w_gate": moe_w_gate.astype(BF16), "b_gate": moe_b_gate[:, None, :],
        "w_up": moe_w_up.astype(BF16), "b_up": moe_b_up[:, None, :],
        "w_down": moe_w_down.astype(BF16), "b_down": moe_b_down[:, None, :],
        "norm_final": norm_final.reshape(1, -1),
    }


def _pick(n, prefs):
    for p in prefs:
        if n % p == 0:
            return p
    return n


def kernel(x_prompt, x_sample, mem_prompt, cache_ckv, cache_krope, cache_mem_k, cache_mem_v, state_rwkv, state_shift, page_table, norm_mix, w_in, mu_shift, rw_w0, rw_w2, rw_a0, rw_a2, rw_g2, rw_k_k, rw_k_a, rw_r_k, rw_lnx_w, rw_lnx_b, mla_q_norm, mla_w_qb, mla_kv_norm, mla_w_kvb, mla_out_norm, w_out, norm_x, norm_mem, xa_wq, xa_wk, xa_wv, xa_wo, norm_ffn, router_w, router_b, moe_w_gate, moe_b_gate, moe_w_up, moe_b_up, moe_w_down, moe_b_down, norm_final):
    assert w_in.shape[0] == 1, "single-layer trunk"
    bp, tp, _ = x_prompt.shape
    bs, ts, _ = x_sample.shape
    n_pages = page_table.shape[1]
    past_len = n_pages * PAGE_SIZE
    assert ts <= RW_CHUNK and tp % RW_CHUNK == 0
    pw = _prep_weights(norm_mix[0], w_in[0], mu_shift[0], rw_w0[0], rw_w2[0], rw_a0[0], rw_a2[0], rw_g2[0],
                       rw_k_k[0], rw_k_a[0], rw_r_k[0], rw_lnx_w[0], rw_lnx_b[0], mla_q_norm[0], mla_w_qb[0],
                       mla_kv_norm[0], mla_w_kvb[0], mla_out_norm[0], w_out[0], norm_x[0], norm_mem[0], xa_wq[0],
                       xa_wk[0], xa_wv[0], xa_wo[0], norm_ffn[0], router_w[0], router_b[0], moe_w_gate[0],
                       moe_b_gate[0], moe_w_up[0], moe_b_up[0], moe_w_down[0], moe_b_down[0], norm_final)
    np_, ns_ = bp * tp, bs * ts
    rt = _pick(ns_ * TOP_K, (512, 256, 128))

    xp2 = x_prompt.reshape(np_, D_MODEL)
    projr_p, qcat_p, kcat_p, ckv_p, krope_p = _mix_in(xp2, tp, 0, pw, _pick(tp, (256, 128, 64)))
    yrw_p, rwkv_p = _rwkv(projr_p.reshape(bp, tp, RW_PROJ), jnp.zeros((bp, RW_PROJ), F32),
                          jnp.zeros((bp, RW_HEADS, 64, 64), F32), pw, RW_CHUNK, _pick(tp, (256, 128, 64)),
                          _pick(tp, (256, 128, 64)) // RW_CHUNK)
    tq = _pick(tp, (256, 128, 64))
    h1_p = _mla_prompt(qcat_p, kcat_p.reshape(bp, tp, KCAT), xp2, yrw_p.reshape(np_, RW_WIDTH), pw, tq,
                       _pick(tp, (512, 256, 128, 64)))
    mk2, mv2 = _mem_kv(mem_prompt.reshape(bp * MEM_TOKENS, D_MODEL), pw, _pick(bp * MEM_TOKENS, (512, 256)))
    h2_p = _xattn(h1_p, mk2.reshape(bp, MEM_TOKENS, X_WIDTH), mv2.reshape(bp, MEM_TOKENS, X_WIDTH), tp, pw,
                  _pick(tp, (512, 256, 128, 64)))
    y_p = _moe_and_final_norm(h2_p, pw, rt, _pick(np_, (512, 256, 128)), _pick(np_, (256, 128)))

    xs2 = x_sample.reshape(ns_, D_MODEL)
    projr_s, qcat_s, _, ckv_s, krope_s = _mix_in(xs2, ts, past_len, pw, _pick(ns_, (256, 128, 64, 8)))
    projr_s3 = projr_s.reshape(bs, ts, RW_PROJ)
    projr_pad = jnp.pad(projr_s3, ((0, 0), (0, RW_CHUNK - ts), (0, 0)))
    yrw_s, rwkv_s = _rwkv(projr_pad, state_shift[0], state_rwkv[0], pw, ts, RW_CHUNK, 1)
    yrw_s = yrw_s[:, :ts].reshape(ns_, RW_WIDTH)
    q4 = qcat_s.reshape(bs, ts, MLA_HEADS, KCAT).transpose(0, 2, 1, 3)
    q_lat = q4[..., :KV_RANK].reshape(bs, MLA_HEADS * ts, KV_RANK)
    o1 = q4[..., KV_RANK:KV_RANK + 128].reshape(bs, MLA_HEADS, ts, MLA_HEADS, 32)
    o2 = q4[..., KV_RANK + 128:].reshape(bs, MLA_HEADS, ts, MLA_HEADS, 32)
    q_rope = jnp.stack([jnp.concatenate([o1[:, h, :, h], o2[:, h, :, h]], axis=-1) for h in range(MLA_HEADS)],
                       axis=1).reshape(bs, MLA_HEADS * ts, QK_ROPE)
    new_ckv = jnp.pad(ckv_s.reshape(bs, ts, KV_RANK), ((0, 0), (0, PAGE_SIZE - ts), (0, 0)))
    new_kr = jnp.pad(krope_s.reshape(bs, ts, QK_ROPE), ((0, 0), (0, PAGE_SIZE - ts), (0, 0)))
    o_s = _paged_attn(q_lat, q_rope, new_ckv, new_kr, cache_ckv, jnp.swapaxes(cache_krope, 2, 3), page_table, ts,
                      _pick(n_pages, (16, 8, 4, 2, 1)))
    o_s = o_s.reshape(bs, MLA_HEADS, ts, KV_RANK).transpose(0, 2, 1, 3).reshape(ns_, MLA_HEADS * KV_RANK)
    h1_s = _out_proj_call(o_s, xs2, yrw_s, pw, _pick(ns_, (256, 128, 64, 8)))
    h2_s = _xattn(h1_s, cache_mem_k[0].reshape(bs, MEM_TOKENS, X_WIDTH), cache_mem_v[0].reshape(bs, MEM_TOKENS, X_WIDTH),
                  ts, pw, ts)
    y_s = _moe_and_final_norm(h2_s, pw, rt, _pick(ns_, (512, 256, 128)), _pick(ns_, (256, 128)))

    return (y_p.reshape(bp, tp, D_MODEL), y_s.reshape(bs, ts, D_MODEL),
            ckv_p.reshape(1, bp, tp, KV_RANK), krope_p.reshape(1, bp, tp, QK_ROPE),
            mk2.reshape(1, bp, MEM_TOKENS, X_HEADS, X_HEAD_DIM), mv2.reshape(1, bp, MEM_TOKENS, X_HEADS, X_HEAD_DIM),
            rwkv_p[None], projr_p.reshape(bp, tp, RW_PROJ)[None, :, -1],
            ckv_s.reshape(1, bs, ts, KV_RANK), krope_s.reshape(1, bs, ts, QK_ROPE),
            rwkv_s[None], projr_s3[None, :, -1])
```

```python
import functools

import jax
import jax.numpy as jnp
from jax import lax
from jax.experimental import pallas as pl
from jax.experimental.pallas import tpu as pltpu

F32 = jnp.float32
BF16 = jnp.bfloat16
I32 = jnp.int32

D_MODEL = 1024
PAGE_SIZE = 128
RW_HEADS = 8
RW_HEAD_DIM = 64
RW_WIDTH = 512
RW_PROJ = 1792
GN_EPS = 64e-5
L2_EPS = 1e-12
MLA_HEADS = 4
QK_NOPE = 128
QK_ROPE = 64
V_HEAD = 128
Q_RANK = 384
KV_RANK = 256
ROPE_THETA = 10000.0
MEM_TOKENS = 256
X_HEADS = 4
X_HEAD_DIM = 128
X_WIDTH = 512
N_EXPERTS = 32
TOP_K = 4
SWIGLU_LIMIT = 7.0
SWIGLU_ALPHA = 1.702
NORM_EPS = 1e-5
MLA_SCALE = (QK_NOPE + QK_ROPE) ** -0.5
X_SCALE = X_HEAD_DIM ** -0.5
NEG_BIG = -1e30

KCAT = KV_RANK + 2 * 128
RW_CHUNK = 64
VMEM_LIMIT_V7X = 56 * 1024 * 1024


def _cparams(sem, vmem_mib=None):
    kw = dict(dimension_semantics=sem)
    if vmem_mib is not None:
        kw["vmem_limit_bytes"] = min(vmem_mib * 1024 * 1024, VMEM_LIMIT_V7X)
    return pltpu.CompilerParams(**kw)


def _nn(a, b):
    return jnp.dot(a, b, preferred_element_type=F32)


def _nt(a, b):
    return lax.dot_general(a, b, (((1,), (1,)), ((), ())), preferred_element_type=F32)


def _tn(a, b):
    return lax.dot_general(a, b, (((0,), (0,)), ((), ())), preferred_element_type=F32)


def _split2(x):
    hi = x.astype(BF16)
    lo = (x - hi.astype(F32)).astype(BF16)
    return hi, lo


def _split3(x):
    p1 = x.astype(BF16)
    r1 = x - p1.astype(F32)
    p2 = r1.astype(BF16)
    p3 = (r1 - p2.astype(F32)).astype(BF16)
    return p1, p2, p3


def _rms(x, g, eps=NORM_EPS):
    return x * lax.rsqrt(jnp.mean(x * x, axis=-1, keepdims=True) + eps) * g


def _sigmoid(x):
    return 1.0 / (1.0 + jnp.exp(-x))


def _full(shape):
    n = len(shape)
    return pl.BlockSpec(shape, lambda *a: (0,) * n)


def _mix_in_kernel(x_ref, nm_ref, wr_ref, wm_ref, qn_ref, wqb_ref, kvn_ref, wk_ref, c4_ref, s4_ref,
                   projr_ref, qcat_ref, kcat_ref, ckv_ref, krope_ref):
    xn = _rms(x_ref[...], nm_ref[...]).astype(BF16)
    projr_ref[...] = _nn(xn, wr_ref[...])
    pm = _nn(xn, wm_ref[...])
    q_a = pm[:, :Q_RANK]
    lat = pm[:, Q_RANK:Q_RANK + KV_RANK]
    k1 = pm[:, 640:768]
    k2 = pm[:, 768:896]
    c4 = c4_ref[...]
    s4 = s4_ref[...]
    ckv = _rms(lat, kvn_ref[...])
    ckv_ref[...] = ckv
    ok1 = k1 * c4 - k2 * s4
    ok2 = k1 * s4 + k2 * c4
    krope_ref[...] = jnp.concatenate([ok1[:, :32], ok2[:, :32]], axis=1)
    kcat_ref[...] = jnp.concatenate([ckv, ok1, ok2], axis=1).astype(BF16)
    qn = _rms(q_a, qn_ref[...]).astype(BF16)
    q = _nn(qn, wqb_ref[...]) * MLA_SCALE
    r1 = q[:, 512:640]
    r2 = q[:, 640:768]
    o1 = r1 * c4 - r2 * s4
    o2 = r1 * s4 + r2 * c4
    lane = lax.broadcasted_iota(I32, o1.shape, 1)
    for h in range(MLA_HEADS):
        ql = _nn(q[:, 128 * h:128 * h + 128].astype(BF16), wk_ref[h])
        mh = (lane >= 32 * h) & (lane < 32 * h + 32)
        qcat_ref[:, KCAT * h:KCAT * (h + 1)] = jnp.concatenate(
            [ql, jnp.where(mh, o1, 0.0), jnp.where(mh, o2, 0.0)], axis=1).astype(BF16)


def _mix_in(x2d, seq_len, pos0, pw, tm):
    n = x2d.shape[0]
    half = QK_ROPE // 2
    inv = ROPE_THETA ** (-jnp.arange(half, dtype=F32) / half)
    pos = (pos0 + jnp.arange(seq_len, dtype=jnp.int32)).astype(F32)
    ang = pos[:, None] * inv[None, :]
    tab_len = max(seq_len, tm)
    c4 = jnp.tile(jnp.cos(ang), (tab_len // seq_len, 4))
    s4 = jnp.tile(jnp.sin(ang), (tab_len // seq_len, 4))
    ntab = tab_len // tm
    row = lambda i: (i, 0)
    tab = lambda i: (i % ntab, 0)
    outs = pl.pallas_call(
        _mix_in_kernel,
        grid=(n // tm,),
        in_specs=[pl.BlockSpec((tm, D_MODEL), row), _full((1, D_MODEL)), _full((D_MODEL, RW_PROJ)),
                  _full((D_MODEL, 896)), _full((1, Q_RANK)), _full((Q_RANK, 768)), _full((1, KV_RANK)),
                  _full((MLA_HEADS, QK_NOPE, KV_RANK)), pl.BlockSpec((tm, 128), tab), pl.BlockSpec((tm, 128), tab)],
        out_specs=[pl.BlockSpec((tm, RW_PROJ), row), pl.BlockSpec((tm, MLA_HEADS * KCAT), row),
                   pl.BlockSpec((tm, KCAT), row), pl.BlockSpec((tm, KV_RANK), row), pl.BlockSpec((tm, QK_ROPE), row)],
        out_shape=[jax.ShapeDtypeStruct((n, RW_PROJ), F32), jax.ShapeDtypeStruct((n, MLA_HEADS * KCAT), BF16),
                   jax.ShapeDtypeStruct((n, KCAT), BF16), jax.ShapeDtypeStruct((n, KV_RANK), F32),
                   jax.ShapeDtypeStruct((n, QK_ROPE), F32)],
        compiler_params=_cparams(("parallel",), 48),
        name="mix_in",
    )(x2d, pw["norm_mix"], pw["w_r"], pw["w_m"], pw["q_norm"], pw["w_qb"], pw["kv_norm"], pw["wk"], c4, s4)
    return outs


def _rwkv_kernel(t_valid, ng, proj_ref, shift_ref, st0_ref, mu_ref, vec_ref, w2a2_ref, g2_ref, ones_ref,
                 y_ref, st_ref, s_scr, carry_scr):
    C = RW_CHUNK
    lc = C.bit_length() - 1
    gr = ng * C
    tb = proj_ref.shape[1]
    npair = RW_HEADS // 2

    @pl.when(pl.program_id(1) == 0)
    def _():
        s_scr[...] = st0_ref[0]
        carry_scr[...] = shift_ref[0]

    mu = mu_ref[...]
    w0 = vec_ref[0:1, :]
    a0 = vec_ref[1:2, :]
    k_k = vec_ref[2:3, :]
    k_a = vec_ref[3:4, :]
    r_k = vec_ref[4:5, :]
    lnx_w = vec_ref[5:6, :]
    lnx_b = vec_ref[6:7, :]
    ones = ones_ref[...]

    def bsum(x):
        hi, lo = _split2(x)
        return _nn(hi, ones) + _nn(lo, ones)

    row = lax.broadcasted_iota(I32, (gr, 1), 0)
    lane128 = lax.broadcasted_iota(I32, (C, 128), 1)
    m0 = lane128 < RW_HEAD_DIM
    rr = lax.broadcasted_iota(I32, (2 * C, 2 * C), 0)
    cc = lax.broadcasted_iota(I32, (2 * C, 2 * C), 1)
    strict = cc < rr
    incl = cc <= rr
    eye = jnp.where(cc == rr, 1.0, 0.0).astype(F32)
    tr = lax.broadcasted_iota(I32, (gr, gr), 0)
    tc = lax.broadcasted_iota(I32, (gr, gr), 1)
    same = jnp.right_shift(tr, lc) == jnp.right_shift(tc, lc)
    tri = jnp.where(same & (tc <= tr), 1.0, 0.0).astype(BF16)
    allc = jnp.where(same, 1.0, 0.0).astype(BF16)

    def stack(x):
        return jnp.concatenate([jnp.where(m0, x, 0.0), jnp.where(m0, 0.0, x)], axis=0).astype(BF16)

    items = [(c, p) for c in range(ng) for p in range(npair)]

    def group(gi, carry):
        r0 = pl.multiple_of(gi * gr, gr)
        x = proj_ref[0, pl.ds(r0, gr), :]
        prev = jnp.where(row == 0, carry_scr[...], pltpu.roll(x, 1, axis=0))
        carry_scr[...] = x[gr - 1:gr, :]
        mixed = x + (prev - x) * mu
        r = mixed[:, 0:512]
        k = mixed[:, 512:1024]
        v = mixed[:, 1024:1536]
        wa = mixed[:, 1536:1664]
        gd = mixed[:, 1664:1792]
        zw = w0 + _nn(jnp.tanh(wa).astype(BF16), w2a2_ref[0])
        sp = jnp.maximum(-zw, 0.0) + jnp.log(1.0 + jnp.exp(-jnp.abs(zw)))
        logw = -jnp.exp(-sp - 0.5)
        a_lr = _sigmoid(a0 + _nn(wa.astype(BF16), w2a2_ref[1]))
        g = _nn(_sigmoid(gd).astype(BF16), g2_ref[...])
        kk = k * k_k
        kk = kk * lax.rsqrt(bsum(kk * kk) + L2_EPS)
        k_mod = k * (1.0 + (a_lr - 1.0) * k_a)
        a_v = -kk
        b_v = kk * a_lr
        if t_valid < C:
            valid = jnp.bitwise_and(row, C - 1) < t_valid
            logw = jnp.where(valid, logw, 0.0)
            a_v = jnp.where(valid, a_v, 0.0)
            b_v = jnp.where(valid, b_v, 0.0)
            k_mod = jnp.where(valid, k_mod, 0.0)
            v = jnp.where(valid, v, 0.0)
        p1, p2, p3 = _split3(logw)
        cum = _nn(tri, p1) + _nn(tri, p2) + _nn(tri, p3)
        tot = _nn(allc, p1) + _nn(allc, p2) + _nn(allc, p3)
        e_neg = jnp.exp(-cum)
        e_rem = jnp.exp(tot - cum)
        r_t = r * jnp.exp(cum)
        a_t = a_v * jnp.exp(cum - logw)
        b_t = b_v * e_neg
        k_t = k_mod * e_neg
        b_g = b_v * e_rem
        k_g = k_mod * e_rem
        gam = jnp.exp(tot)

        def blk(arr, c, p):
            return stack(arr[c * C:(c + 1) * C, 128 * p:128 * p + 128])

        a2 = [blk(a_t, c, p) for c, p in items]
        r2 = [blk(r_t, c, p) for c, p in items]
        v2 = [blk(v, c, p) for c, p in items]
        gram = [_nt(jnp.concatenate([a2[i], r2[i]], axis=0),
                    jnp.concatenate([blk(b_t, c, p), blk(k_t, c, p)], axis=0)) for i, (c, p) in enumerate(items)]
        tinv = [eye + jnp.where(strict, gm[:2 * C, :2 * C], 0.0) for gm in gram]
        nb = [jnp.where(strict, gm[:2 * C, :2 * C], 0.0).astype(BF16) for gm in gram]
        akrk = [jnp.concatenate([jnp.where(strict, gm[:2 * C, 2 * C:], 0.0), jnp.where(incl, gm[2 * C:, 2 * C:], 0.0)],
                                axis=0).astype(BF16) for gm in gram]
        arb = [jnp.where(incl, gm[2 * C:, :2 * C], 0.0).astype(BF16) for gm in gram]
        pw = [_nn(n, n) for n in nb]
        for it in range(1, lc):
            pb = [q.astype(BF16) for q in pw]
            if it < lc - 1:
                res = [_nn(pb[i], jnp.concatenate([tinv[i].astype(BF16), pb[i]], axis=1)) for i in range(len(items))]
                tinv = [tinv[i] + res[i][:, :2 * C] for i in range(len(items))]
                pw = [q[:, 2 * C:] for q in res]
            else:
                tinv = [tinv[i] + _nn(pb[i], tinv[i].astype(BF16)) for i in range(len(items))]
        av = [_nn(akrk[i], v2[i]) for i in range(len(items))]
        tw = [_nn(tinv[i].astype(BF16), jnp.concatenate([a2[i], av[i][:2 * C].astype(BF16)], axis=1))
              for i in range(len(items))]

        yrows = []
        for c in range(ng):
            idx = [c * npair + p for p in range(npair)]
            s_old = [s_scr[p] for p in range(npair)]
            sb = [s.astype(BF16) for s in s_old]
            u = [_nt(tw[i][:, :2 * C].astype(BF16), sb[p]) + tw[i][:, 2 * C:] for p, i in enumerate(idx)]
            ub = [q.astype(BF16) for q in u]
            y2 = [_nt(r2[i], sb[p]) + _nn(arb[i], ub[p]) + av[i][2 * C:] for p, i in enumerate(idx)]
            for p, i in enumerate(idx):
                s_scr[p] = s_old[p] * gam[c * C:c * C + 1, 128 * p:128 * p + 128] + _tn(
                    jnp.concatenate([ub[p], v2[i]], axis=0),
                    jnp.concatenate([blk(b_g, c, p), blk(k_g, c, p)], axis=0))
            yrows.append(jnp.concatenate([q[:C] + q[C:] for q in y2], axis=1))
        y = yrows[0] if ng == 1 else jnp.concatenate(yrows, axis=0)
        d = y - bsum(y) * (1.0 / RW_HEAD_DIM)
        var = bsum(d * d) * (1.0 / RW_HEAD_DIM)
        yn = d * lax.rsqrt(var + GN_EPS) * lnx_w + lnx_b
        bonus = bsum(r * k_mod * r_k) * v
        y_ref[0, pl.ds(r0, gr), :] = ((yn + bonus) * g).astype(BF16)
        return carry

    lax.fori_loop(0, tb // gr, group, 0)

    @pl.when(pl.program_id(1) == pl.num_programs(1) - 1)
    def _():
        st_ref[0] = s_scr[...]


def _pair_state(s):
    b = s.shape[0]
    s = s.reshape(b, 4, 2, 64, 64)
    z = jnp.zeros_like(s[:, :, 0])
    top = jnp.concatenate([s[:, :, 0], z], axis=-1)
    bot = jnp.concatenate([z, s[:, :, 1]], axis=-1)
    return jnp.concatenate([top, bot], axis=-2)


def _unpair_state(s2):
    b = s2.shape[0]
    return jnp.stack([s2[:, :, :64, :64], s2[:, :, 64:, 64:]], axis=2).reshape(b, RW_HEADS, 64, 64)


def _rwkv(proj_r3, shift_prev, state0, pw, t_valid, tb, ng):
    b, t, _ = proj_r3.shape
    y, st = pl.pallas_call(
        functools.partial(_rwkv_kernel, t_valid, ng),
        grid=(b, t // tb),
        in_specs=[pl.BlockSpec((1, tb, RW_PROJ), lambda i, j: (i, j, 0)),
                  pl.BlockSpec((1, 1, RW_PROJ), lambda i, j: (i, 0, 0)),
                  pl.BlockSpec((1, 4, 128, 128), lambda i, j: (i, 0, 0, 0)),
                  _full((1, RW_PROJ)), _full((8, RW_WIDTH)), _full((2, 128, RW_WIDTH)), _full((128, RW_WIDTH)),
                  _full((RW_WIDTH, RW_WIDTH))],
        out_specs=[pl.BlockSpec((1, tb, RW_WIDTH), lambda i, j: (i, j, 0)),
                   pl.BlockSpec((1, 4, 128, 128), lambda i, j: (i, 0, 0, 0))],
        out_shape=[jax.ShapeDtypeStruct((b, t, RW_WIDTH), BF16), jax.ShapeDtypeStruct((b, 4, 128, 128), F32)],
        scratch_shapes=[pltpu.VMEM((4, 128, 128), F32), pltpu.VMEM((1, RW_PROJ), F32)],
        compiler_params=_cparams(("parallel", "arbitrary"), 40),
        name="rwkv7",
    )(proj_r3, shift_prev.reshape(b, 1, RW_PROJ), _pair_state(state0), pw["mu"], pw["rw_vec"], pw["w2a2"], pw["g2"],
      pw["ones64"])
    return y, _unpair_state(st)


def _out_proj(o_heads, x, yrw, wv_ref, on_ref, wo_ref):
    ys = [_nn(o_heads[h].astype(BF16), wv_ref[h]) for h in range(MLA_HEADS)]
    y_mla = _rms(jnp.concatenate(ys, axis=1), on_ref[...]).astype(BF16)
    return x + _nn(yrw, wo_ref[0:RW_WIDTH, :]) + _nn(y_mla, wo_ref[RW_WIDTH:, :])


def _mla_prompt_kernel(tk, q_ref, k_ref, x_ref, yrw_ref, wv_ref, on_ref, wo_ref, o_ref, acc_scr, m_scr, l_scr):
    tq = q_ref.shape[0]
    i = pl.program_id(1)
    acc_scr[...] = jnp.zeros_like(acc_scr)
    m_scr[...] = jnp.full_like(m_scr, NEG_BIG)
    l_scr[...] = jnp.zeros_like(l_scr)
    qpos = i * tq + lax.broadcasted_iota(I32, (tq, tk), 0)
    kidx = lax.broadcasted_iota(I32, (tq, tk), 1)
    n_kv = (i * tq + tq + tk - 1) // tk

    def body(j, carry):
        k0 = pl.multiple_of(j * tk, tk)
        kc = k_ref[0, pl.ds(k0, tk), :]
        vc = kc[:, :KV_RANK]
        mask = (kidx + j * tk) <= qpos

        def qk(h):
            return _nt(q_ref[:, KCAT * h:KCAT * (h + 1)], kc)

        def soft_pv(h, s):
            s = jnp.where(mask, s, NEG_BIG)
            m_old = m_scr[h]
            m_new = jnp.maximum(m_old, jnp.max(s, axis=1, keepdims=True))
            p = jnp.exp(s - m_new)
            corr = jnp.exp(m_old - m_new)
            l_scr[h] = l_scr[h] * corr + jnp.sum(p, axis=1, keepdims=True)
            m_scr[h] = m_new
            acc_scr[h] = acc_scr[h] * corr + _nn(p.astype(BF16), vc)

        s_prev = qk(0)
        for h in range(1, MLA_HEADS):
            s_next = qk(h)
            soft_pv(h - 1, s_prev)
            s_prev = s_next
        soft_pv(MLA_HEADS - 1, s_prev)
        return carry

    lax.fori_loop(0, n_kv, body, 0)
    o_heads = [acc_scr[h] / l_scr[h] for h in range(MLA_HEADS)]
    o_ref[...] = _out_proj(o_heads, x_ref[...], yrw_ref[...], wv_ref, on_ref, wo_ref)


def _mla_prompt(qcat, kcat3, x2d, yrw2d, pw, tq, tk):
    b, t, _ = kcat3.shape
    nq = t // tq
    row = lambda i, j: (i * nq + j, 0)
    return pl.pallas_call(
        functools.partial(_mla_prompt_kernel, tk),
        grid=(b, nq),
        in_specs=[pl.BlockSpec((tq, MLA_HEADS * KCAT), row), pl.BlockSpec((1, t, KCAT), lambda i, j: (i, 0, 0)),
                  pl.BlockSpec((tq, D_MODEL), row), pl.BlockSpec((tq, RW_WIDTH), row),
                  _full((MLA_HEADS, KV_RANK, V_HEAD)), _full((1, 512)), _full((D_MODEL, D_MODEL))],
        out_specs=pl.BlockSpec((tq, D_MODEL), row),
        out_shape=jax.ShapeDtypeStruct((b * t, D_MODEL), F32),
        scratch_shapes=[pltpu.VMEM((MLA_HEADS, tq, KV_RANK), F32), pltpu.VMEM((MLA_HEADS, tq, 1), F32),
                        pltpu.VMEM((MLA_HEADS, tq, 1), F32)],
        compiler_params=_cparams(("parallel", "arbitrary"), 40),
        name="mla_prompt",
    )(qcat, kcat3, x2d, yrw2d, pw["wv"], pw["out_norm"], pw["w_out"])


def _out_proj_kernel(o_ref, x_ref, yrw_ref, wv_ref, on_ref, wo_ref, out_ref):
    o_heads = [o_ref[:, KV_RANK * h:KV_RANK * (h + 1)] for h in range(MLA_HEADS)]
    out_ref[...] = _out_proj(o_heads, x_ref[...], yrw_ref[...], wv_ref, on_ref, wo_ref)


def _out_proj_call(o_lat, x2d, yrw2d, pw, tm):
    n = x2d.shape[0]
    row = lambda i: (i, 0)
    return pl.pallas_call(
        _out_proj_kernel,
        grid=(n // tm,),
        in_specs=[pl.BlockSpec((tm, MLA_HEADS * KV_RANK), row), pl.BlockSpec((tm, D_MODEL), row),
                  pl.BlockSpec((tm, RW_WIDTH), row), _full((MLA_HEADS, KV_RANK, V_HEAD)), _full((1, 512)),
                  _full((D_MODEL, D_MODEL))],
        out_specs=pl.BlockSpec((tm, D_MODEL), row),
        out_shape=jax.ShapeDtypeStruct((n, D_MODEL), F32),
        compiler_params=_cparams(("parallel",), 32),
        name="mla_out_proj",
    )(o_lat, x2d, yrw2d, pw["wv"], pw["out_norm"], pw["w_out"])


def _paged_attn_kernel(pg, t_new, pt_ref, ql_ref, qr_ref, nck_ref, nkr_ref, *rest):
    ck_refs = rest[:pg]
    kr_refs = rest[pg:2 * pg]
    o_ref = rest[2 * pg]
    m_scr, l_scr, acc_scr, kall, krall = rest[2 * pg + 1:]
    g = pl.program_id(1)

    @pl.when(g == 0)
    def _():
        m_scr[...] = jnp.full_like(m_scr, NEG_BIG)
        l_scr[...] = jnp.zeros_like(l_scr)
        acc_scr[...] = jnp.zeros_like(acc_scr)

    ql = ql_ref[0]
    qr = qr_ref[0]

    def merge(s, v):
        m_old = m_scr[...]
        m_new = jnp.maximum(m_old, jnp.max(s, axis=1, keepdims=True))
        p = jnp.exp(s - m_new)
        corr = jnp.exp(m_old - m_new)
        l_scr[...] = l_scr[...] * corr + jnp.sum(p, axis=1, keepdims=True)
        acc_scr[...] = acc_scr[...] * corr + _nn(p.astype(BF16), v)
        m_scr[...] = m_new

    for j in range(pg):
        kall[PAGE_SIZE * j:PAGE_SIZE * (j + 1), :] = ck_refs[j][...].astype(BF16)
        krall[:, PAGE_SIZE * j:PAGE_SIZE * (j + 1)] = kr_refs[j][...].astype(BF16)
    ka = kall[...]
    merge(_nt(ql, ka) + _nn(qr, krall[...]), ka)

    @pl.when(g == pl.num_programs(1) - 1)
    def _():
        nck = nck_ref[0].astype(BF16)
        s = _nt(ql, nck) + _nt(qr, nkr_ref[0].astype(BF16))
        rows = lax.broadcasted_iota(I32, s.shape, 0)
        cols = lax.broadcasted_iota(I32, s.shape, 1)
        s = jnp.where(cols <= rows % t_new, s, NEG_BIG)
        merge(s, nck)
        o_ref[0] = acc_scr[...] / l_scr[...]


def _paged_attn(q_lat, q_rope, new_ckv, new_kr, cache_ckv, cache_krope_t, page_table, t_new, pg):
    b, n_pages = page_table.shape
    nq = q_lat.shape[1]
    ck_specs = [pl.BlockSpec((None, None, PAGE_SIZE, KV_RANK),
                             (lambda i, g, pt, j=j: (0, pt[i * n_pages + g * pg + j], 0, 0))) for j in range(pg)]
    kr_specs = [pl.BlockSpec((None, None, QK_ROPE, PAGE_SIZE),
                             (lambda i, g, pt, j=j: (0, pt[i * n_pages + g * pg + j], 0, 0))) for j in range(pg)]
    bmap = lambda i, g, pt: (i, 0, 0)
    gs = pltpu.PrefetchScalarGridSpec(
        num_scalar_prefetch=1,
        grid=(b, n_pages // pg),
        in_specs=[pl.BlockSpec((1, nq, KV_RANK), bmap), pl.BlockSpec((1, nq, QK_ROPE), bmap),
                  pl.BlockSpec((1, PAGE_SIZE, KV_RANK), bmap), pl.BlockSpec((1, PAGE_SIZE, QK_ROPE), bmap)]
        + ck_specs + kr_specs,
        out_specs=pl.BlockSpec((1, nq, KV_RANK), bmap),
        scratch_shapes=[pltpu.VMEM((nq, 1), F32), pltpu.VMEM((nq, 1), F32), pltpu.VMEM((nq, KV_RANK), F32),
                        pltpu.VMEM((pg * PAGE_SIZE, KV_RANK), BF16), pltpu.VMEM((QK_ROPE, pg * PAGE_SIZE), BF16)],
    )
    return pl.pallas_call(
        functools.partial(_paged_attn_kernel, pg, t_new),
        grid_spec=gs,
        out_shape=jax.ShapeDtypeStruct((b, nq, KV_RANK), F32),
        compiler_params=_cparams(("parallel", "arbitrary"), 32),
        name="mla_paged",
    )(page_table.reshape(-1), q_lat, q_rope, new_ckv, new_kr, *([cache_ckv] * pg), *([cache_krope_t] * pg))


def _mem_kv_kernel(m_ref, nm_ref, wk_ref, wv_ref, k_ref, v_ref):
    mn = _rms(m_ref[...], nm_ref[...]).astype(BF16)
    k_ref[...] = _nn(mn, wk_ref[...])
    v_ref[...] = _nn(mn, wv_ref[...])


def _mem_kv(mem2d, pw, tm):
    n = mem2d.shape[0]
    row = lambda i: (i, 0)
    return pl.pallas_call(
        _mem_kv_kernel,
        grid=(n // tm,),
        in_specs=[pl.BlockSpec((tm, D_MODEL), row), _full((1, D_MODEL)), _full((D_MODEL, X_WIDTH)),
                  _full((D_MODEL, X_WIDTH))],
        out_specs=[pl.BlockSpec((tm, X_WIDTH), row), pl.BlockSpec((tm, X_WIDTH), row)],
        out_shape=[jax.ShapeDtypeStruct((n, X_WIDTH), F32)] * 2,
        compiler_params=_cparams(("parallel",), 32),
        name="mem_kv",
    )(mem2d, pw["norm_mem"], pw["xa_wk"], pw["xa_wv"])


def _xattn_kernel(h_ref, mk_ref, mv_ref, nx_ref, wq_ref, wo_ref, o_ref):
    h = h_ref[...]
    xn = _rms(h, nx_ref[...]).astype(BF16)
    q = (_nn(xn, wq_ref[...]) * X_SCALE).astype(BF16)
    mk = mk_ref[0].astype(BF16)
    mv = mv_ref[0].astype(BF16)
    outs = []
    for hh in range(X_HEADS):
        sl = slice(X_HEAD_DIM * hh, X_HEAD_DIM * (hh + 1))
        s = _nt(q[:, sl], mk[:, sl])
        p = jnp.exp(s - jnp.max(s, axis=1, keepdims=True))
        outs.append(_nn(p.astype(BF16), mv[:, sl]) / jnp.sum(p, axis=1, keepdims=True))
    o_ref[...] = h + _nn(jnp.concatenate(outs, axis=1).astype(BF16), wo_ref[...])


def _xattn(h2d, mem_k, mem_v, seq_len, pw, tm):
    n = h2d.shape[0]
    nt = seq_len // tm
    row = lambda i, j: (i * nt + j, 0)
    bm = lambda i, j: (i, 0, 0)
    return pl.pallas_call(
        _xattn_kernel,
        grid=(n // seq_len, nt),
        in_specs=[pl.BlockSpec((tm, D_MODEL), row), pl.BlockSpec((1, MEM_TOKENS, X_WIDTH), bm),
                  pl.BlockSpec((1, MEM_TOKENS, X_WIDTH), bm), _full((1, D_MODEL)), _full((D_MODEL, X_WIDTH)),
                  _full((X_WIDTH, D_MODEL))],
        out_specs=pl.BlockSpec((tm, D_MODEL), row),
        out_shape=jax.ShapeDtypeStruct((n, D_MODEL), F32),
        compiler_params=_cparams(("parallel", "arbitrary"), 32),
        name="mem_xattn",
    )(h2d, mem_k, mem_v, pw["norm_x"], pw["xa_wq"], pw["xa_wo"])


def _router_kernel(h_ref, nf_ref, rw_ref, rb_ref, loc_ref, g_ref, cb_ref, nt_ref, cnt_ref, cnt_scr):
    tm = h_ref.shape[0]

    @pl.when(pl.program_id(0) == 0)
    def _():
        cnt_scr[...] = jnp.zeros_like(cnt_scr)

    xn = _rms(h_ref[...], nf_ref[...])
    x_hi, x_lo = _split2(xn)
    w_hi, w_lo = _split2(rw_ref[...])
    logits = _nt(w_hi, x_hi) + _nt(w_hi, x_lo) + _nt(w_lo, x_hi) + rb_ref[...]
    eid = lax.broadcasted_iota(I32, logits.shape, 0)
    vals, hots = [], []
    l = logits
    for k in range(TOP_K):
        m = jnp.max(l, axis=0, keepdims=True)
        idx = jnp.min(jnp.where(l == m, eid, N_EXPERTS), axis=0, keepdims=True)
        hot = eid == idx
        vals.append(m)
        hots.append(hot)
        l = jnp.where(hot, -jnp.inf, l)
    ex = [jnp.exp(vv - vals[0]) for vv in vals]
    den = ex[0] + ex[1] + ex[2] + ex[3]
    for k in range(TOP_K):
        g_ref[k:k + 1, :] = ex[k] / den
    hot_all = jnp.where(hots[0] | hots[1] | hots[2] | hots[3], 1.0, 0.0).astype(F32)
    ts = lax.broadcasted_iota(I32, (tm, tm), 0)
    tt = lax.broadcasted_iota(I32, (tm, tm), 1)
    upper = jnp.where(ts < tt, 1.0, 0.0).astype(BF16)
    before = _nn(hot_all.astype(BF16), upper)
    n_t = jnp.sum(hot_all, axis=1, keepdims=True)
    er = lax.broadcasted_iota(I32, (N_EXPERTS, N_EXPERTS), 0)
    ec = lax.broadcasted_iota(I32, (N_EXPERTS, N_EXPERTS), 1)
    lower = jnp.where(ec < er, 1.0, 0.0).astype(BF16)
    off = _nn(lower, jnp.broadcast_to(n_t, (N_EXPERTS, 128)).astype(BF16))[:, 0:1]
    pos = before + off
    for k in range(TOP_K):
        loc_ref[k:k + 1, :] = jnp.sum(jnp.where(hots[k], pos, 0.0), axis=0, keepdims=True).astype(I32)
    cb_ref[0] = jnp.broadcast_to(cnt_scr[...], (N_EXPERTS, 128)).astype(I32)
    nt_ref[0] = jnp.broadcast_to(n_t, (N_EXPERTS, 128)).astype(I32)
    cnt_scr[...] = cnt_scr[...] + n_t
    cnt_ref[...] = jnp.broadcast_to(cnt_scr[...], cnt_ref.shape).astype(I32)


def _router(h2d, pw, tm):
    n = h2d.shape[0]
    nt = n // tm
    col = lambda i: (0, i)
    per_tile = lambda i: (i, 0, 0)
    return pl.pallas_call(
        _router_kernel,
        grid=(nt,),
        in_specs=[pl.BlockSpec((tm, D_MODEL), lambda i: (i, 0)), _full((1, D_MODEL)), _full((N_EXPERTS, D_MODEL)),
                  _full((N_EXPERTS, 1))],
        out_specs=[pl.BlockSpec((TOP_K, tm), col), pl.BlockSpec((TOP_K, tm), col),
                   pl.BlockSpec((1, N_EXPERTS, 128), per_tile), pl.BlockSpec((1, N_EXPERTS, 128), per_tile),
                   _full((N_EXPERTS, 128))],
        out_shape=[jax.ShapeDtypeStruct((TOP_K, n), I32), jax.ShapeDtypeStruct((TOP_K, n), F32),
                   jax.ShapeDtypeStruct((nt, N_EXPERTS, 128), I32), jax.ShapeDtypeStruct((nt, N_EXPERTS, 128), I32),
                   jax.ShapeDtypeStruct((N_EXPERTS, 128), I32)],
        scratch_shapes=[pltpu.VMEM((N_EXPERTS, 1), F32)],
        compiler_params=_cparams(("arbitrary",), 32),
        name="moe_router",
    )(h2d, pw["norm_ffn"], pw["router_wt"], pw["router_b"])


RUN_CHUNK = 64


def _run_copies(n, src, dst, start_fn):
    nfull = lax.shift_right_logical(n, 6)

    def body(c, carry):
        start_fn(src + c * RUN_CHUNK, dst + c * RUN_CHUNK, RUN_CHUNK)
        return carry

    lax.fori_loop(0, nfull, body, 0)
    base = lax.shift_left(nfull, 6)
    low = n - base
    for b in (32, 16, 8, 4, 2, 1):
        @pl.when(jnp.bitwise_and(low, b) != 0)
        def _():
            o = base + jnp.bitwise_and(low, RUN_CHUNK - 2 * b)
            start_fn(src + o, dst + o, b)


def _dispatch_kernel(rt, pe_ref, pd_ref, h_ref, nf_ref, loc_ref, run_ref, xd_ref, xs3, zbuf, sems, zsem):
    tm = h_ref.shape[0]
    na = TOP_K * tm
    i = pl.program_id(0)
    slot = lax.rem(i, 2)

    @pl.when(i == 0)
    def _():
        zbuf[...] = jnp.zeros_like(zbuf)
        for e in range(N_EXPERTS):
            @pl.when(pd_ref[e] > 0)
            def _():
                st = pl.multiple_of((pe_ref[e] - rt) * 8, rt * 8)
                cp = pltpu.make_async_copy(zbuf, xd_ref.at[pl.ds(st, rt * 8)], zsem)
                cp.start()
                cp.wait()

    def drain(s):
        pltpu.make_async_copy(xs3.at[0], xd_ref.at[pl.ds(0, na * 8)], sems.at[s]).wait()

    xn = _rms(h_ref[...], nf_ref[...]).astype(BF16)
    rows = lax.broadcasted_iota(I32, (na, tm), 0)
    hit = rows == loc_ref[0:1, :]
    for k in range(1, TOP_K):
        hit = hit | (rows == loc_ref[k:k + 1, :])
    xs = _nn(jnp.where(hit, 1.0, 0.0).astype(BF16), xn)

    @pl.when(i > 1)
    def _():
        drain(slot)

    for j in range(D_MODEL // 128):
        xs3[slot, pl.ds(j, na, stride=8), :] = xs[:, 128 * j:128 * (j + 1)]

    def start(sr, ds_, ln):
        pltpu.make_async_copy(xs3.at[slot, pl.ds(pl.multiple_of(sr * 8, 8), ln * 8)],
                              xd_ref.at[pl.ds(pl.multiple_of(ds_ * 8, 8), ln * 8)], sems.at[slot]).start()

    for e in range(N_EXPERTS):
        _run_copies(run_ref[0, e], run_ref[1, e], run_ref[2, e], start)

    @pl.when(i == pl.num_programs(0) - 1)
    def _():
        drain(slot)

        @pl.when(i > 0)
        def _():
            drain(1 - slot)


def _dispatch(h2d, loc, runs, pad_end, padded, cap, rt, pw, tm):
    n = h2d.shape[0]
    gs = pltpu.PrefetchScalarGridSpec(
        num_scalar_prefetch=2,
        grid=(n // tm,),
        in_specs=[pl.BlockSpec((tm, D_MODEL), lambda i, *_: (i, 0)), pl.BlockSpec((1, D_MODEL), lambda i, *_: (0, 0)),
                  pl.BlockSpec((TOP_K, tm), lambda i, *_: (0, i)),
                  pl.BlockSpec((None, 3, N_EXPERTS), lambda i, *_: (i, 0, 0), memory_space=pltpu.SMEM)],
        out_specs=pl.BlockSpec(memory_space=pl.ANY),
        scratch_shapes=[pltpu.VMEM((2, TOP_K * tm * 8, 128), F32), pltpu.VMEM((rt * 8, 128), F32),
                        pltpu.SemaphoreType.DMA((2,)), pltpu.SemaphoreType.DMA],
    )
    return pl.pallas_call(
        functools.partial(_dispatch_kernel, rt),
        grid_spec=gs,
        out_shape=jax.ShapeDtypeStruct((cap * 8, 128), F32),
        compiler_params=_cparams(("arbitrary",), 40),
        name="moe_dispatch",
    )(pad_end, padded, h2d, pw["norm_ffn"], loc, runs)


def _expert_kernel(te_ref, nu_ref, x_ref, wg_ref, bg_ref, wu_ref, bu_ref, wd_ref, bd_ref, y_ref):
    @pl.when(pl.program_id(0) < nu_ref[0])
    def _():
        rt = y_ref.shape[0] // 8
        nj = D_MODEL // 128
        x = jnp.concatenate([x_ref[pl.ds(j, rt, stride=8), :] for j in range(nj)], axis=1).astype(BF16)
        gate = jnp.minimum(_nn(x, wg_ref[0]) + bg_ref[0], SWIGLU_LIMIT)
        up = jnp.clip(_nn(x, wu_ref[0]) + bu_ref[0], -SWIGLU_LIMIT, SWIGLU_LIMIT)
        hid = (up + 1.0) * (gate * _sigmoid(SWIGLU_ALPHA * gate))
        y = _nn(hid.astype(BF16), wd_ref[0]) + bd_ref[0]
        for j in range(nj):
            y_ref[pl.ds(j, rt, stride=8), :] = y[:, 128 * j:128 * (j + 1)]


def _experts(x_disp, tile_expert, n_used, pw, rt):
    cap = x_disp.shape[0] // 8
    d_ff = pw["w_gate"].shape[2]
    tile = lambda i, te, nu: (jnp.minimum(i, nu[0] - 1), 0)
    wsel = lambda i, te, nu: (te[jnp.minimum(i, nu[0] - 1)], 0, 0)
    gs = pltpu.PrefetchScalarGridSpec(
        num_scalar_prefetch=2,
        grid=(cap // rt,),
        in_specs=[pl.BlockSpec((rt * 8, 128), tile),
                  pl.BlockSpec((1, D_MODEL, d_ff), wsel), pl.BlockSpec((1, 1, d_ff), wsel),
                  pl.BlockSpec((1, D_MODEL, d_ff), wsel), pl.BlockSpec((1, 1, d_ff), wsel),
                  pl.BlockSpec((1, d_ff, D_MODEL), wsel), pl.BlockSpec((1, 1, D_MODEL), wsel)],
        out_specs=pl.BlockSpec((rt * 8, 128), tile),
    )
    return pl.pallas_call(
        _expert_kernel,
        grid_spec=gs,
        out_shape=jax.ShapeDtypeStruct((cap * 8, 128), F32),
        compiler_params=_cparams(("arbitrary",), 56),
        name="moe_experts",
    )(tile_expert, n_used, x_disp, pw["w_gate"], pw["b_gate"], pw["w_up"], pw["b_up"], pw["w_down"], pw["b_down"])


def _combine_kernel(h_ref, g_ref, loc_ref, nfin_ref, run_ref, runn_ref, yd_ref, o_ref, ys3, sems):
    tm = h_ref.shape[0]
    na = TOP_K * tm
    i = pl.program_id(0)
    slot = lax.rem(i, 2)

    def issue(rref, s):
        def start(sr, ds_, ln):
            pltpu.make_async_copy(yd_ref.at[pl.ds(pl.multiple_of(ds_ * 8, 8), ln * 8)],
                                  ys3.at[s, pl.ds(pl.multiple_of(sr * 8, 8), ln * 8)], sems.at[s]).start()
        for e in range(N_EXPERTS):
            _run_copies(rref[0, e], rref[1, e], rref[2, e], start)

    @pl.when(i == 0)
    def _():
        issue(run_ref, 0)

    @pl.when(i + 1 < pl.num_programs(0))
    def _():
        issue(runn_ref, 1 - slot)

    pltpu.make_async_copy(yd_ref.at[pl.ds(0, na * 8)], ys3.at[0], sems.at[slot]).wait()
    ys = jnp.concatenate([ys3[slot, pl.ds(j, na, stride=8), :] for j in range(D_MODEL // 128)], axis=1).astype(BF16)
    cols = lax.broadcasted_iota(I32, (tm, na), 1)
    g = g_ref[...]
    loc = loc_ref[...]
    gm = jnp.where(cols == loc[:, 0:1], g[:, 0:1], 0.0)
    for k in range(1, TOP_K):
        gm = gm + jnp.where(cols == loc[:, k:k + 1], g[:, k:k + 1], 0.0)
    o_ref[...] = _rms(h_ref[...] + _nn(gm.astype(BF16), ys), nfin_ref[...])


def _combine(h2d, gates_t, loc_t, runs, y_disp, pw, tm):
    n = h2d.shape[0]
    nt = n // tm
    row = lambda i: (i, 0)
    return pl.pallas_call(
        _combine_kernel,
        grid=(nt,),
        in_specs=[pl.BlockSpec((tm, D_MODEL), row), pl.BlockSpec((tm, TOP_K), row), pl.BlockSpec((tm, TOP_K), row),
                  _full((1, D_MODEL)),
                  pl.BlockSpec((None, 3, N_EXPERTS), lambda i: (i, 0, 0), memory_space=pltpu.SMEM),
                  pl.BlockSpec((None, 3, N_EXPERTS), lambda i: (jnp.minimum(i + 1, nt - 1), 0, 0),
                               memory_space=pltpu.SMEM),
                  pl.BlockSpec(memory_space=pl.ANY)],
        out_specs=pl.BlockSpec((tm, D_MODEL), row),
        out_shape=jax.ShapeDtypeStruct((n, D_MODEL), F32),
        scratch_shapes=[pltpu.VMEM((2, TOP_K * tm * 8, 128), F32), pltpu.SemaphoreType.DMA((2,))],
        compiler_params=_cparams(("arbitrary",), 40),
        name="moe_combine",
    )(h2d, gates_t, loc_t, pw["norm_final"], runs, runs, y_disp)


def _moe_and_final_norm(h2d, pw, rt, tm):
    n = h2d.shape[0]
    loc, gates, cnt_before, n_tile, counts = _router(h2d, pw, tm)
    counts = counts[:, 0]
    padded = (counts + rt - 1) // rt * rt
    pad_end = jnp.cumsum(padded).astype(I32)
    pad_start = pad_end - padded
    n_tiles = (n * TOP_K) // rt + N_EXPERTS
    cap = n_tiles * rt
    n_te = n_tile[:, :, 0]
    runs = jnp.stack([n_te, jnp.cumsum(n_te, axis=1) - n_te, pad_start[None, :] + cnt_before[:, :, 0]], axis=1).astype(I32)
    tile_expert = jnp.minimum(jnp.sum((pad_end[None, :] <= (jnp.arange(n_tiles, dtype=I32) * rt)[:, None]).astype(I32),
                                      axis=1), N_EXPERTS - 1).astype(I32)
    n_used = (pad_end[-1:] // rt).astype(I32)
    x_disp = _dispatch(h2d, loc, runs, pad_end, padded.astype(I32), cap, rt, pw, tm)
    y_disp = _experts(x_disp, tile_expert, n_used, pw, rt)
    return _combine(h2d, gates.T, loc.T, runs, y_disp, pw, tm)


def _prep_weights(norm_mix, w_in, mu_shift, rw_w0, rw_w2, rw_a0, rw_a2, rw_g2, rw_k_k, rw_k_a, rw_r_k, rw_lnx_w,
                  rw_lnx_b, mla_q_norm, mla_w_qb, mla_kv_norm, mla_w_kvb, mla_out_norm, w_out, norm_x, norm_mem,
                  xa_wq, xa_wk, xa_wv, xa_wo, norm_ffn, router_w, router_b, moe_w_gate, moe_b_gate, moe_w_up,
                  moe_b_up, moe_w_down, moe_b_down, norm_final):
    w_m = w_in[:, RW_PROJ:]
    qb = mla_w_qb.reshape(Q_RANK, MLA_HEADS, QK_NOPE + QK_ROPE)
    w_kv = mla_w_kvb.reshape(KV_RANK, MLA_HEADS, QK_NOPE + V_HEAD)
    z64 = jnp.zeros((64, RW_WIDTH), F32)
    blk = jnp.arange(RW_WIDTH) // RW_HEAD_DIM
    return {
        "norm_mix": norm_mix.reshape(1, -1),
        "w_r": w_in[:, :RW_PROJ].astype(BF16),
        "w_m": jnp.concatenate([w_m[:, :640], jnp.tile(w_m[:, 640:672], (1, 4)), jnp.tile(w_m[:, 672:704], (1, 4))],
                               axis=1).astype(BF16),
        "q_norm": mla_q_norm.reshape(1, -1),
        "w_qb": jnp.concatenate([qb[:, :, :QK_NOPE].reshape(Q_RANK, -1), qb[:, :, QK_NOPE:QK_NOPE + 32].reshape(Q_RANK, -1),
                                 qb[:, :, QK_NOPE + 32:].reshape(Q_RANK, -1)], axis=1).astype(BF16),
        "kv_norm": mla_kv_norm.reshape(1, -1),
        "wk": jnp.transpose(w_kv[:, :, :QK_NOPE], (1, 2, 0)).astype(BF16),
        "wv": jnp.transpose(w_kv[:, :, QK_NOPE:], (1, 0, 2)).astype(BF16),
        "out_norm": mla_out_norm.reshape(1, -1),
        "w_out": w_out.astype(BF16),
        "mu": mu_shift.reshape(1, -1),
        "rw_vec": jnp.stack([rw_w0, rw_a0, rw_k_k, rw_k_a, rw_r_k.reshape(-1), rw_lnx_w, rw_lnx_b,
                             jnp.zeros_like(rw_w0)], axis=0),
        "w2a2": jnp.stack([jnp.concatenate([rw_w2, z64], axis=0), jnp.concatenate([z64, rw_a2], axis=0)]).astype(BF16),
        "g2": rw_g2.astype(BF16),
        "ones64": (blk[:, None] == blk[None, :]).astype(BF16),
        "norm_x": norm_x.reshape(1, -1),
        "norm_mem": norm_mem.reshape(1, -1),
        "xa_wq": xa_wq.astype(BF16), "xa_wk": xa_wk.astype(BF16), "xa_wv": xa_wv.astype(BF16),
        "xa_wo": xa_wo.astype(BF16),
        "norm_ffn": norm_ffn.reshape(1, -1),
        "router_wt": router_w.T,
        "router_b": router_b.reshape(-1, 1),
        "w_gate": moe_w_gate.astype(BF16), "b_gate": moe_b_gate[:, None, :],
        "w_up": moe_w_up.astype(BF16), "b_up": moe_b_up[:, None, :],
        "w_down": moe_w_down.astype(BF16), "b_down": moe_b_down[:, None, :],
        "norm_final": norm_final.reshape(1, -1),
    }


def _pick(n, prefs):
    for p in prefs:
        if n % p == 0:
            return p
    return n


def kernel(x_prompt, x_sample, mem_prompt, cache_ckv, cache_krope, cache_mem_k, cache_mem_v, state_rwkv, state_shift, page_table, norm_mix, w_in, mu_shift, rw_w0, rw_w2, rw_a0, rw_a2, rw_g2, rw_k_k, rw_k_a, rw_r_k, rw_lnx_w, rw_lnx_b, mla_q_norm, mla_w_qb, mla_kv_norm, mla_w_kvb, mla_out_norm, w_out, norm_x, norm_mem, xa_wq, xa_wk, xa_wv, xa_wo, norm_ffn, router_w, router_b, moe_w_gate, moe_b_gate, moe_w_up, moe_b_up, moe_w_down, moe_b_down, norm_final):
    assert w_in.shape[0] == 1, "single-layer trunk"
    bp, tp, _ = x_prompt.shape
    bs, ts, _ = x_sample.shape
    n_pages = page_table.shape[1]
    past_len = n_pages * PAGE_SIZE
    assert ts <= RW_CHUNK and tp % RW_CHUNK == 0
    pw = _prep_weights(norm_mix[0], w_in[0], mu_shift[0], rw_w0[0], rw_w2[0], rw_a0[0], rw_a2[0], rw_g2[0],
                       rw_k_k[0], rw_k_a[0], rw_r_k[0], rw_lnx_w[0], rw_lnx_b[0], mla_q_norm[0], mla_w_qb[0],
                       mla_kv_norm[0], mla_w_kvb[0], mla_out_norm[0], w_out[0], norm_x[0], norm_mem[0], xa_wq[0],
                       xa_wk[0], xa_wv[0], xa_wo[0], norm_ffn[0], router_w[0], router_b[0], moe_w_gate[0],
                       moe_b_gate[0], moe_w_up[0], moe_b_up[0], moe_w_down[0], moe_b_down[0], norm_final)
    np_, ns_ = bp * tp, bs * ts
    rt = _pick(ns_ * TOP_K, (512, 256, 128))

    xp2 = x_prompt.reshape(np_, D_MODEL)
    projr_p, qcat_p, kcat_p, ckv_p, krope_p = _mix_in(xp2, tp, 0, pw, _pick(tp, (256, 128, 64)))
    yrw_p, rwkv_p = _rwkv(projr_p.reshape(bp, tp, RW_PROJ), jnp.zeros((bp, RW_PROJ), F32),
                          jnp.zeros((bp, RW_HEADS, 64, 64), F32), pw, RW_CHUNK, _pick(tp, (256, 128, 64)),
                          _pick(tp, (256, 128, 64)) // RW_CHUNK)
    tq = _pick(tp, (256, 128, 64))
    h1_p = _mla_prompt(qcat_p, kcat_p.reshape(bp, tp, KCAT), xp2, yrw_p.reshape(np_, RW_WIDTH), pw, tq,
                       _pick(tp, (512, 256, 128, 64)))
    mk2, mv2 = _mem_kv(mem_prompt.reshape(bp * MEM_TOKENS, D_MODEL), pw, _pick(bp * MEM_TOKENS, (512, 256)))
    h2_p = _xattn(h1_p, mk2.reshape(bp, MEM_TOKENS, X_WIDTH), mv2.reshape(bp, MEM_TOKENS, X_WIDTH), tp, pw,
                  _pick(tp, (512, 256, 128, 64)))
    y_p = _moe_and_final_norm(h2_p, pw, rt, _pick(np_, (256, 128)))

    xs2 = x_sample.reshape(ns_, D_MODEL)
    projr_s, qcat_s, _, ckv_s, krope_s = _mix_in(xs2, ts, past_len, pw, _pick(ns_, (256, 128, 64, 8)))
    projr_s3 = projr_s.reshape(bs, ts, RW_PROJ)
    projr_pad = jnp.pad(projr_s3, ((0, 0), (0, RW_CHUNK - ts), (0, 0)))
    yrw_s, rwkv_s = _rwkv(projr_pad, state_shift[0], state_rwkv[0], pw, ts, RW_CHUNK, 1)
    yrw_s = yrw_s[:, :ts].reshape(ns_, RW_WIDTH)
    q4 = qcat_s.reshape(bs, ts, MLA_HEADS, KCAT).transpose(0, 2, 1, 3)
    q_lat = q4[..., :KV_RANK].reshape(bs, MLA_HEADS * ts, KV_RANK)
    o1 = q4[..., KV_RANK:KV_RANK + 128].reshape(bs, MLA_HEADS, ts, MLA_HEADS, 32)
    o2 = q4[..., KV_RANK + 128:].reshape(bs, MLA_HEADS, ts, MLA_HEADS, 32)
    q_rope = jnp.stack([jnp.concatenate([o1[:, h, :, h], o2[:, h, :, h]], axis=-1) for h in range(MLA_HEADS)],
                       axis=1).reshape(bs, MLA_HEADS * ts, QK_ROPE)
    new_ckv = jnp.pad(ckv_s.reshape(bs, ts, KV_RANK), ((0, 0), (0, PAGE_SIZE - ts), (0, 0)))
    new_kr = jnp.pad(krope_s.reshape(bs, ts, QK_ROPE), ((0, 0), (0, PAGE_SIZE - ts), (0, 0)))
    o_s = _paged_attn(q_lat, q_rope, new_ckv, new_kr, cache_ckv, jnp.swapaxes(cache_krope, 2, 3), page_table, ts,
                      _pick(n_pages, (16, 8, 4, 2, 1)))
    o_s = o_s.reshape(bs, MLA_HEADS, ts, KV_RANK).transpose(0, 2, 1, 3).reshape(ns_, MLA_HEADS * KV_RANK)
    h1_s = _out_proj_call(o_s, xs2, yrw_s, pw, _pick(ns_, (256, 128, 64, 8)))
    h2_s = _xattn(h1_s, cache_mem_k[0].reshape(bs, MEM_TOKENS, X_WIDTH), cache_mem_v[0].reshape(bs, MEM_TOKENS, X_WIDTH),
                  ts, pw, ts)
    y_s = _moe_and_final_norm(h2_s, pw, rt, _pick(ns_, (256, 128)))

    return (y_p.reshape(bp, tp, D_MODEL), y_s.reshape(bs, ts, D_MODEL),
            ckv_p.reshape(1, bp, tp, KV_RANK), krope_p.reshape(1, bp, tp, QK_ROPE),
            mk2.reshape(1, bp, MEM_TOKENS, X_HEADS, X_HEAD_DIM), mv2.reshape(1, bp, MEM_TOKENS, X_HEADS, X_HEAD_DIM),
            rwkv_p[None], projr_p.reshape(bp, tp, RW_PROJ)[None, :, -1],
            ckv_s.reshape(1, bs, ts, KV_RANK), krope_s.reshape(1, bs, ts, QK_ROPE),
            rwkv_s[None], projr_s3[None, :, -1])
```

```python
import functools

import jax
import jax.numpy as jnp
from jax import lax
from jax.experimental import pallas as pl
from jax.experimental.pallas import tpu as pltpu

F32 = jnp.float32
BF16 = jnp.bfloat16
I32 = jnp.int32

D_MODEL = 1024
PAGE_SIZE = 128
RW_HEADS = 8
RW_HEAD_DIM = 64
RW_WIDTH = 512
RW_PROJ = 1792
GN_EPS = 64e-5
L2_EPS = 1e-12
MLA_HEADS = 4
QK_NOPE = 128
QK_ROPE = 64
V_HEAD = 128
Q_RANK = 384
KV_RANK = 256
ROPE_THETA = 10000.0
MEM_TOKENS = 256
X_HEADS = 4
X_HEAD_DIM = 128
X_WIDTH = 512
N_EXPERTS = 32
TOP_K = 4
SWIGLU_LIMIT = 7.0
SWIGLU_ALPHA = 1.702
NORM_EPS = 1e-5
MLA_SCALE = (QK_NOPE + QK_ROPE) ** -0.5
X_SCALE = X_HEAD_DIM ** -0.5
NEG_BIG = -1e30

KCAT = KV_RANK + 2 * 128
RW_CHUNK = 64
VMEM_LIMIT_V7X = 56 * 1024 * 1024


def _cparams(sem, vmem_mib=None):
    kw = dict(dimension_semantics=sem)
    if vmem_mib is not None:
        kw["vmem_limit_bytes"] = min(vmem_mib * 1024 * 1024, VMEM_LIMIT_V7X)
    return pltpu.CompilerParams(**kw)


def _nn(a, b):
    return jnp.dot(a, b, preferred_element_type=F32)


def _nt(a, b):
    return lax.dot_general(a, b, (((1,), (1,)), ((), ())), preferred_element_type=F32)


def _tn(a, b):
    return lax.dot_general(a, b, (((0,), (0,)), ((), ())), preferred_element_type=F32)


def _split2(x):
    hi = x.astype(BF16)
    lo = (x - hi.astype(F32)).astype(BF16)
    return hi, lo


def _split3(x):
    p1 = x.astype(BF16)
    r1 = x - p1.astype(F32)
    p2 = r1.astype(BF16)
    p3 = (r1 - p2.astype(F32)).astype(BF16)
    return p1, p2, p3


def _rms(x, g, eps=NORM_EPS):
    return x * lax.rsqrt(jnp.mean(x * x, axis=-1, keepdims=True) + eps) * g


def _sigmoid(x):
    return 1.0 / (1.0 + jnp.exp(-x))


def _full(shape):
    n = len(shape)
    return pl.BlockSpec(shape, lambda *a: (0,) * n)


def _mix_in_kernel(x_ref, nm_ref, wr_ref, wm_ref, qn_ref, wqb_ref, kvn_ref, wk_ref, c4_ref, s4_ref,
                   projr_ref, qcat_ref, kcat_ref, ckv_ref, krope_ref):
    xn = _rms(x_ref[...], nm_ref[...]).astype(BF16)
    projr_ref[...] = _nn(xn, wr_ref[...])
    pm = _nn(xn, wm_ref[...])
    q_a = pm[:, :Q_RANK]
    lat = pm[:, Q_RANK:Q_RANK + KV_RANK]
    k1 = pm[:, 640:768]
    k2 = pm[:, 768:896]
    c4 = c4_ref[...]
    s4 = s4_ref[...]
    ckv = _rms(lat, kvn_ref[...])
    ckv_ref[...] = ckv
    ok1 = k1 * c4 - k2 * s4
    ok2 = k1 * s4 + k2 * c4
    krope_ref[...] = jnp.concatenate([ok1[:, :32], ok2[:, :32]], axis=1)
    kcat_ref[...] = jnp.concatenate([ckv, ok1, ok2], axis=1).astype(BF16)
    qn = _rms(q_a, qn_ref[...]).astype(BF16)
    q = _nn(qn, wqb_ref[...]) * MLA_SCALE
    r1 = q[:, 512:640]
    r2 = q[:, 640:768]
    o1 = r1 * c4 - r2 * s4
    o2 = r1 * s4 + r2 * c4
    lane = lax.broadcasted_iota(I32, o1.shape, 1)
    for h in range(MLA_HEADS):
        ql = _nn(q[:, 128 * h:128 * h + 128].astype(BF16), wk_ref[h])
        mh = (lane >= 32 * h) & (lane < 32 * h + 32)
        qcat_ref[:, KCAT * h:KCAT * (h + 1)] = jnp.concatenate(
            [ql, jnp.where(mh, o1, 0.0), jnp.where(mh, o2, 0.0)], axis=1).astype(BF16)


def _mix_in(x2d, seq_len, pos0, pw, tm):
    n = x2d.shape[0]
    half = QK_ROPE // 2
    inv = ROPE_THETA ** (-jnp.arange(half, dtype=F32) / half)
    pos = (pos0 + jnp.arange(seq_len, dtype=jnp.int32)).astype(F32)
    ang = pos[:, None] * inv[None, :]
    tab_len = max(seq_len, tm)
    c4 = jnp.tile(jnp.cos(ang), (tab_len // seq_len, 4))
    s4 = jnp.tile(jnp.sin(ang), (tab_len // seq_len, 4))
    ntab = tab_len // tm
    row = lambda i: (i, 0)
    tab = lambda i: (i % ntab, 0)
    outs = pl.pallas_call(
        _mix_in_kernel,
        grid=(n // tm,),
        in_specs=[pl.BlockSpec((tm, D_MODEL), row), _full((1, D_MODEL)), _full((D_MODEL, RW_PROJ)),
                  _full((D_MODEL, 896)), _full((1, Q_RANK)), _full((Q_RANK, 768)), _full((1, KV_RANK)),
                  _full((MLA_HEADS, QK_NOPE, KV_RANK)), pl.BlockSpec((tm, 128), tab), pl.BlockSpec((tm, 128), tab)],
        out_specs=[pl.BlockSpec((tm, RW_PROJ), row), pl.BlockSpec((tm, MLA_HEADS * KCAT), row),
                   pl.BlockSpec((tm, KCAT), row), pl.BlockSpec((tm, KV_RANK), row), pl.BlockSpec((tm, QK_ROPE), row)],
        out_shape=[jax.ShapeDtypeStruct((n, RW_PROJ), F32), jax.ShapeDtypeStruct((n, MLA_HEADS * KCAT), BF16),
                   jax.ShapeDtypeStruct((n, KCAT), BF16), jax.ShapeDtypeStruct((n, KV_RANK), F32),
                   jax.ShapeDtypeStruct((n, QK_ROPE), F32)],
        compiler_params=_cparams(("parallel",), 48),
        name="mix_in",
    )(x2d, pw["norm_mix"], pw["w_r"], pw["w_m"], pw["q_norm"], pw["w_qb"], pw["kv_norm"], pw["wk"], c4, s4)
    return outs


def _rwkv_kernel(t_valid, ng, nb, proj_ref, shift_ref, st0_ref, mu_ref, vec_ref, w2a2_ref, g2_ref, ones_ref,
                 y_ref, st_ref, s_scr, carry_scr):
    C = RW_CHUNK
    lc = C.bit_length() - 1
    nseg = ng * nb
    gr = nseg * C
    tb = proj_ref.shape[1]
    npair = RW_HEADS // 2

    if nb == 1:
        @pl.when(pl.program_id(1) == 0)
        def _():
            s_scr[...] = st0_ref[0]
            carry_scr[...] = shift_ref[0]

    mu = mu_ref[...]
    w0 = vec_ref[0:1, :]
    a0 = vec_ref[1:2, :]
    k_k = vec_ref[2:3, :]
    k_a = vec_ref[3:4, :]
    r_k = vec_ref[4:5, :]
    lnx_w = vec_ref[5:6, :]
    lnx_b = vec_ref[6:7, :]
    ones = ones_ref[...]

    def bsum(x):
        return _nn(x.astype(BF16), ones)

    row = lax.broadcasted_iota(I32, (gr, 1), 0)
    lane128 = lax.broadcasted_iota(I32, (C, 128), 1)
    m0 = lane128 < RW_HEAD_DIM
    rr = lax.broadcasted_iota(I32, (2 * C, 2 * C), 0)
    cc = lax.broadcasted_iota(I32, (2 * C, 2 * C), 1)
    strict = cc < rr
    incl = cc <= rr
    eye = jnp.where(cc == rr, 1.0, 0.0).astype(F32)
    tr = lax.broadcasted_iota(I32, (gr, gr), 0)
    tc = lax.broadcasted_iota(I32, (gr, gr), 1)
    same = jnp.right_shift(tr, lc) == jnp.right_shift(tc, lc)
    tri = jnp.where(same & (tc <= tr), 1.0, 0.0).astype(BF16)
    allc = jnp.where(same, 1.0, 0.0).astype(BF16)

    def stack(x):
        return jnp.concatenate([jnp.where(m0, x, 0.0), jnp.where(m0, 0.0, x)], axis=0).astype(BF16)

    items = [(c, p) for c in range(nseg) for p in range(npair)]

    def group(gi, carry):
        r0 = pl.multiple_of(gi * gr, gr)
        if nb == 1:
            x = proj_ref[0, pl.ds(r0, gr), :]
            prev = jnp.where(row == 0, carry_scr[...], pltpu.roll(x, 1, axis=0))
            carry_scr[...] = x[gr - 1:gr, :]
        else:
            x = proj_ref[...].reshape(gr, RW_PROJ)
            prev = pltpu.roll(x, 1, axis=0)
            for bi in range(nb):
                prev = jnp.where(row == bi * C, shift_ref[bi], prev)
        mixed = x + (prev - x) * mu
        r = mixed[:, 0:512]
        k = mixed[:, 512:1024]
        v = mixed[:, 1024:1536]
        wa = mixed[:, 1536:1664]
        gd = mixed[:, 1664:1792]
        zw = w0 + _nn(jnp.tanh(wa).astype(BF16), w2a2_ref[0])
        sp = jnp.maximum(-zw, 0.0) + jnp.log(1.0 + jnp.exp(-jnp.abs(zw)))
        logw = -jnp.exp(-sp - 0.5)
        a_lr = _sigmoid(a0 + _nn(wa.astype(BF16), w2a2_ref[1]))
        g = _nn(_sigmoid(gd).astype(BF16), g2_ref[...])
        kk = k * k_k
        kk = kk * lax.rsqrt(bsum(kk * kk) + L2_EPS)
        k_mod = k * (1.0 + (a_lr - 1.0) * k_a)
        a_v = -kk
        b_v = kk * a_lr
        if t_valid < C:
            valid = jnp.bitwise_and(row, C - 1) < t_valid
            logw = jnp.where(valid, logw, 0.0)
            a_v = jnp.where(valid, a_v, 0.0)
            b_v = jnp.where(valid, b_v, 0.0)
            k_mod = jnp.where(valid, k_mod, 0.0)
            v = jnp.where(valid, v, 0.0)
        p1, p2, p3 = _split3(logw)
        cum = _nn(tri, p1) + _nn(tri, p2) + _nn(tri, p3)
        tot = jnp.concatenate([jnp.broadcast_to(cum[c * C + C - 1:c * C + C, :], (C, RW_WIDTH)) for c in range(nseg)],
                              axis=0)
        e_neg = jnp.exp(-cum)
        e_rem = jnp.exp(tot - cum)
        r_t = r * jnp.exp(cum)
        a_t = a_v * jnp.exp(cum - logw)
        b_t = b_v * e_neg
        k_t = k_mod * e_neg
        b_g = b_v * e_rem
        k_g = k_mod * e_rem
        gam = jnp.exp(tot)

        def blk(arr, c, p):
            return stack(arr[c * C:(c + 1) * C, 128 * p:128 * p + 128])

        a2 = [blk(a_t, c, p) for c, p in items]
        r2 = [blk(r_t, c, p) for c, p in items]
        v2 = [blk(v, c, p) for c, p in items]
        gram = [_nt(jnp.concatenate([a2[i], r2[i]], axis=0),
                    jnp.concatenate([blk(b_t, c, p), blk(k_t, c, p)], axis=0)) for i, (c, p) in enumerate(items)]
        tinv = [eye + jnp.where(strict, gm[:2 * C, :2 * C], 0.0) for gm in gram]
        nab = [jnp.where(strict, gm[:2 * C, :2 * C], 0.0).astype(BF16) for gm in gram]
        akrk = [jnp.concatenate([jnp.where(strict, gm[:2 * C, 2 * C:], 0.0), jnp.where(incl, gm[2 * C:, 2 * C:], 0.0)],
                                axis=0).astype(BF16) for gm in gram]
        arb = [jnp.where(incl, gm[2 * C:, :2 * C], 0.0).astype(BF16) for gm in gram]
        pw = [_nn(n, n) for n in nab]
        for it in range(1, lc):
            pb = [q.astype(BF16) for q in pw]
            if it < lc - 1:
                res = [_nn(pb[i], jnp.concatenate([tinv[i].astype(BF16), pb[i]], axis=1)) for i in range(len(items))]
                tinv = [tinv[i] + res[i][:, :2 * C] for i in range(len(items))]
                pw = [q[:, 2 * C:] for q in res]
            else:
                tinv = [tinv[i] + _nn(pb[i], tinv[i].astype(BF16)) for i in range(len(items))]
        av = [_nn(akrk[i], v2[i]) for i in range(len(items))]
        tw = [_nn(tinv[i].astype(BF16), jnp.concatenate([a2[i], av[i][:2 * C].astype(BF16)], axis=1))
              for i in range(len(items))]

        yrows = []
        for c in range(nseg):
            idx = [c * npair + p for p in range(npair)]
            s_old = [s_scr[p] if nb == 1 else st0_ref[c, p] for p in range(npair)]
            sb = [s.astype(BF16) for s in s_old]
            u = [_nt(tw[i][:, :2 * C].astype(BF16), sb[p]) + tw[i][:, 2 * C:] for p, i in enumerate(idx)]
            ub = [q.astype(BF16) for q in u]
            y2 = [_nt(r2[i], sb[p]) + _nn(arb[i], ub[p]) + av[i][2 * C:] for p, i in enumerate(idx)]
            for p, i in enumerate(idx):
                s_new = s_old[p] * gam[c * C:c * C + 1, 128 * p:128 * p + 128] + _tn(
                    jnp.concatenate([ub[p], v2[i]], axis=0),
                    jnp.concatenate([blk(b_g, c, p), blk(k_g, c, p)], axis=0))
                if nb == 1:
                    s_scr[p] = s_new
                else:
                    st_ref[c, p] = s_new
            yrows.append(jnp.concatenate([q[:C] + q[C:] for q in y2], axis=1))
        y = yrows[0] if nseg == 1 else jnp.concatenate(yrows, axis=0)
        d = y - bsum(y) * (1.0 / RW_HEAD_DIM)
        var = bsum(d * d) * (1.0 / RW_HEAD_DIM)
        yn = d * lax.rsqrt(var + GN_EPS) * lnx_w + lnx_b
        bonus = bsum(r * k_mod * r_k) * v
        out = ((yn + bonus) * g).astype(BF16)
        if nb == 1:
            y_ref[0, pl.ds(r0, gr), :] = out
        else:
            y_ref[...] = out.reshape(nb, C, RW_WIDTH)
        return carry

    if nb == 1:
        lax.fori_loop(0, tb // gr, group, 0)

        @pl.when(pl.program_id(1) == pl.num_programs(1) - 1)
        def _():
            st_ref[0] = s_scr[...]
    else:
        group(0, 0)


def _pair_state(s):
    b = s.shape[0]
    s = s.reshape(b, 4, 2, 64, 64)
    z = jnp.zeros_like(s[:, :, 0])
    top = jnp.concatenate([s[:, :, 0], z], axis=-1)
    bot = jnp.concatenate([z, s[:, :, 1]], axis=-1)
    return jnp.concatenate([top, bot], axis=-2)


def _unpair_state(s2):
    b = s2.shape[0]
    return jnp.stack([s2[:, :, :64, :64], s2[:, :, 64:, 64:]], axis=2).reshape(b, RW_HEADS, 64, 64)


def _rwkv(proj_r3, shift_prev, state0, pw, t_valid, tb, ng, nb=1):
    b, t, _ = proj_r3.shape
    assert nb == 1 or (ng == 1 and tb == t == RW_CHUNK and b % nb == 0)
    y, st = pl.pallas_call(
        functools.partial(_rwkv_kernel, t_valid, ng, nb),
        grid=(b // nb, t // tb),
        in_specs=[pl.BlockSpec((nb, tb, RW_PROJ), lambda i, j: (i, j, 0)),
                  pl.BlockSpec((nb, 1, RW_PROJ), lambda i, j: (i, 0, 0)),
                  pl.BlockSpec((nb, 4, 128, 128), lambda i, j: (i, 0, 0, 0)),
                  _full((1, RW_PROJ)), _full((8, RW_WIDTH)), _full((2, 128, RW_WIDTH)), _full((128, RW_WIDTH)),
                  _full((RW_WIDTH, RW_WIDTH))],
        out_specs=[pl.BlockSpec((nb, tb, RW_WIDTH), lambda i, j: (i, j, 0)),
                   pl.BlockSpec((nb, 4, 128, 128), lambda i, j: (i, 0, 0, 0))],
        out_shape=[jax.ShapeDtypeStruct((b, t, RW_WIDTH), BF16), jax.ShapeDtypeStruct((b, 4, 128, 128), F32)],
        scratch_shapes=[pltpu.VMEM((4, 128, 128), F32), pltpu.VMEM((1, RW_PROJ), F32)],
        compiler_params=_cparams(("parallel", "arbitrary"), 40),
        name="rwkv7",
    )(proj_r3, shift_prev.reshape(b, 1, RW_PROJ), _pair_state(state0), pw["mu"], pw["rw_vec"], pw["w2a2"], pw["g2"],
      pw["ones64"])
    return y, _unpair_state(st)


def _out_proj(o_heads, x, yrw, wv_ref, on_ref, wo_ref):
    ys = [_nn(o_heads[h].astype(BF16), wv_ref[h]) for h in range(MLA_HEADS)]
    y_mla = _rms(jnp.concatenate(ys, axis=1), on_ref[...]).astype(BF16)
    return x + _nn(yrw, wo_ref[0:RW_WIDTH, :]) + _nn(y_mla, wo_ref[RW_WIDTH:, :])


def _mla_prompt_kernel(tk, q_ref, k_ref, x_ref, yrw_ref, wv_ref, on_ref, wo_ref, o_ref, acc_scr, m_scr, l_scr):
    tq = q_ref.shape[0]
    i = pl.program_id(1)
    acc_scr[...] = jnp.zeros_like(acc_scr)
    m_scr[...] = jnp.full_like(m_scr, NEG_BIG)
    l_scr[...] = jnp.zeros_like(l_scr)
    qpos = i * tq + lax.broadcasted_iota(I32, (tq, tk), 0)
    kidx = lax.broadcasted_iota(I32, (tq, tk), 1)
    n_kv = (i * tq + tq + tk - 1) // tk

    def body(j, carry):
        k0 = pl.multiple_of(j * tk, tk)
        kc = k_ref[0, pl.ds(k0, tk), :]
        vc = kc[:, :KV_RANK]
        mask = (kidx + j * tk) <= qpos

        def qk(h):
            return _nt(q_ref[:, KCAT * h:KCAT * (h + 1)], kc)

        def soft_pv(h, s):
            s = jnp.where(mask, s, NEG_BIG)
            m_old = m_scr[h]
            m_new = jnp.maximum(m_old, jnp.max(s, axis=1, keepdims=True))
            p = jnp.exp(s - m_new)
            corr = jnp.exp(m_old - m_new)
            l_scr[h] = l_scr[h] * corr + jnp.sum(p, axis=1, keepdims=True)
            m_scr[h] = m_new
            acc_scr[h] = acc_scr[h] * corr + _nn(p.astype(BF16), vc)

        s_prev = qk(0)
        for h in range(1, MLA_HEADS):
            s_next = qk(h)
            soft_pv(h - 1, s_prev)
            s_prev = s_next
        soft_pv(MLA_HEADS - 1, s_prev)
        return carry

    lax.fori_loop(0, n_kv, body, 0)
    o_heads = [acc_scr[h] / l_scr[h] for h in range(MLA_HEADS)]
    o_ref[...] = _out_proj(o_heads, x_ref[...], yrw_ref[...], wv_ref, on_ref, wo_ref)


def _mla_prompt(qcat, kcat3, x2d, yrw2d, pw, tq, tk):
    b, t, _ = kcat3.shape
    nq = t // tq
    row = lambda i, j: (i * nq + j, 0)
    return pl.pallas_call(
        functools.partial(_mla_prompt_kernel, tk),
        grid=(b, nq),
        in_specs=[pl.BlockSpec((tq, MLA_HEADS * KCAT), row), pl.BlockSpec((1, t, KCAT), lambda i, j: (i, 0, 0)),
                  pl.BlockSpec((tq, D_MODEL), row), pl.BlockSpec((tq, RW_WIDTH), row),
                  _full((MLA_HEADS, KV_RANK, V_HEAD)), _full((1, 512)), _full((D_MODEL, D_MODEL))],
        out_specs=pl.BlockSpec((tq, D_MODEL), row),
        out_shape=jax.ShapeDtypeStruct((b * t, D_MODEL), F32),
        scratch_shapes=[pltpu.VMEM((MLA_HEADS, tq, KV_RANK), F32), pltpu.VMEM((MLA_HEADS, tq, 1), F32),
                        pltpu.VMEM((MLA_HEADS, tq, 1), F32)],
        compiler_params=_cparams(("parallel", "arbitrary"), 40),
        name="mla_prompt",
    )(qcat, kcat3, x2d, yrw2d, pw["wv"], pw["out_norm"], pw["w_out"])


def _out_proj_kernel(o_ref, x_ref, yrw_ref, wv_ref, on_ref, wo_ref, out_ref):
    o_heads = [o_ref[:, KV_RANK * h:KV_RANK * (h + 1)] for h in range(MLA_HEADS)]
    out_ref[...] = _out_proj(o_heads, x_ref[...], yrw_ref[...], wv_ref, on_ref, wo_ref)


def _out_proj_call(o_lat, x2d, yrw2d, pw, tm):
    n = x2d.shape[0]
    row = lambda i: (i, 0)
    return pl.pallas_call(
        _out_proj_kernel,
        grid=(n // tm,),
        in_specs=[pl.BlockSpec((tm, MLA_HEADS * KV_RANK), row), pl.BlockSpec((tm, D_MODEL), row),
                  pl.BlockSpec((tm, RW_WIDTH), row), _full((MLA_HEADS, KV_RANK, V_HEAD)), _full((1, 512)),
                  _full((D_MODEL, D_MODEL))],
        out_specs=pl.BlockSpec((tm, D_MODEL), row),
        out_shape=jax.ShapeDtypeStruct((n, D_MODEL), F32),
        compiler_params=_cparams(("parallel",), 32),
        name="mla_out_proj",
    )(o_lat, x2d, yrw2d, pw["wv"], pw["out_norm"], pw["w_out"])


def _paged_attn_kernel(pg, t_new, pt_ref, ql_ref, qr_ref, nck_ref, nkr_ref, *rest):
    ck_refs = rest[:pg]
    kr_refs = rest[pg:2 * pg]
    o_ref = rest[2 * pg]
    m_scr, l_scr, acc_scr, kall, krall = rest[2 * pg + 1:]
    g = pl.program_id(1)

    @pl.when(g == 0)
    def _():
        m_scr[...] = jnp.full_like(m_scr, NEG_BIG)
        l_scr[...] = jnp.zeros_like(l_scr)
        acc_scr[...] = jnp.zeros_like(acc_scr)

    ql = ql_ref[0]
    qr = qr_ref[0]

    def merge(s, v):
        m_old = m_scr[...]
        m_new = jnp.maximum(m_old, jnp.max(s, axis=1, keepdims=True))
        p = jnp.exp(s - m_new)
        corr = jnp.exp(m_old - m_new)
        l_scr[...] = l_scr[...] * corr + jnp.sum(p, axis=1, keepdims=True)
        acc_scr[...] = acc_scr[...] * corr + _nn(p.astype(BF16), v)
        m_scr[...] = m_new

    for j in range(pg):
        kall[PAGE_SIZE * j:PAGE_SIZE * (j + 1), :] = ck_refs[j][...].astype(BF16)
        krall[:, PAGE_SIZE * j:PAGE_SIZE * (j + 1)] = kr_refs[j][...].astype(BF16)
    ka = kall[...]
    merge(_nt(ql, ka) + _nn(qr, krall[...]), ka)

    @pl.when(g == pl.num_programs(1) - 1)
    def _():
        nck = nck_ref[0].astype(BF16)
        s = _nt(ql, nck) + _nt(qr, nkr_ref[0].astype(BF16))
        rows = lax.broadcasted_iota(I32, s.shape, 0)
        cols = lax.broadcasted_iota(I32, s.shape, 1)
        s = jnp.where(cols <= rows % t_new, s, NEG_BIG)
        merge(s, nck)
        o_ref[0] = acc_scr[...] / l_scr[...]


def _paged_attn(q_lat, q_rope, new_ckv, new_kr, cache_ckv, cache_krope_t, page_table, t_new, pg):
    b, n_pages = page_table.shape
    nq = q_lat.shape[1]
    ck_specs = [pl.BlockSpec((None, None, PAGE_SIZE, KV_RANK),
                             (lambda i, g, pt, j=j: (0, pt[i * n_pages + g * pg + j], 0, 0))) for j in range(pg)]
    kr_specs = [pl.BlockSpec((None, None, QK_ROPE, PAGE_SIZE),
                             (lambda i, g, pt, j=j: (0, pt[i * n_pages + g * pg + j], 0, 0))) for j in range(pg)]
    bmap = lambda i, g, pt: (i, 0, 0)
    gs = pltpu.PrefetchScalarGridSpec(
        num_scalar_prefetch=1,
        grid=(b, n_pages // pg),
        in_specs=[pl.BlockSpec((1, nq, KV_RANK), bmap), pl.BlockSpec((1, nq, QK_ROPE), bmap),
                  pl.BlockSpec((1, PAGE_SIZE, KV_RANK), bmap), pl.BlockSpec((1, PAGE_SIZE, QK_ROPE), bmap)]
        + ck_specs + kr_specs,
        out_specs=pl.BlockSpec((1, nq, KV_RANK), bmap),
        scratch_shapes=[pltpu.VMEM((nq, 1), F32), pltpu.VMEM((nq, 1), F32), pltpu.VMEM((nq, KV_RANK), F32),
                        pltpu.VMEM((pg * PAGE_SIZE, KV_RANK), BF16), pltpu.VMEM((QK_ROPE, pg * PAGE_SIZE), BF16)],
    )
    return pl.pallas_call(
        functools.partial(_paged_attn_kernel, pg, t_new),
        grid_spec=gs,
        out_shape=jax.ShapeDtypeStruct((b, nq, KV_RANK), F32),
        compiler_params=_cparams(("parallel", "arbitrary"), 32),
        name="mla_paged",
    )(page_table.reshape(-1), q_lat, q_rope, new_ckv, new_kr, *([cache_ckv] * pg), *([cache_krope_t] * pg))


def _mem_kv_kernel(m_ref, nm_ref, wk_ref, wv_ref, k_ref, v_ref):
    mn = _rms(m_ref[...], nm_ref[...]).astype(BF16)
    k_ref[...] = _nn(mn, wk_ref[...])
    v_ref[...] = _nn(mn, wv_ref[...])


def _mem_kv(mem2d, pw, tm):
    n = mem2d.shape[0]
    row = lambda i: (i, 0)
    return pl.pallas_call(
        _mem_kv_kernel,
        grid=(n // tm,),
        in_specs=[pl.BlockSpec((tm, D_MODEL), row), _full((1, D_MODEL)), _full((D_MODEL, X_WIDTH)),
                  _full((D_MODEL, X_WIDTH))],
        out_specs=[pl.BlockSpec((tm, X_WIDTH), row), pl.BlockSpec((tm, X_WIDTH), row)],
        out_shape=[jax.ShapeDtypeStruct((n, X_WIDTH), F32)] * 2,
        compiler_params=_cparams(("parallel",), 32),
        name="mem_kv",
    )(mem2d, pw["norm_mem"], pw["xa_wk"], pw["xa_wv"])


def _xattn_kernel(h_ref, mk_ref, mv_ref, nx_ref, wq_ref, wo_ref, o_ref):
    h = h_ref[...]
    xn = _rms(h, nx_ref[...]).astype(BF16)
    q = (_nn(xn, wq_ref[...]) * X_SCALE).astype(BF16)
    mk = mk_ref[0].astype(BF16)
    mv = mv_ref[0].astype(BF16)
    outs = []
    for hh in range(X_HEADS):
        sl = slice(X_HEAD_DIM * hh, X_HEAD_DIM * (hh + 1))
        s = _nt(q[:, sl], mk[:, sl])
        p = jnp.exp(s - jnp.max(s, axis=1, keepdims=True))
        outs.append(_nn(p.astype(BF16), mv[:, sl]) / jnp.sum(p, axis=1, keepdims=True))
    o_ref[...] = h + _nn(jnp.concatenate(outs, axis=1).astype(BF16), wo_ref[...])


def _xattn(h2d, mem_k, mem_v, seq_len, pw, tm):
    n = h2d.shape[0]
    nt = seq_len // tm
    row = lambda i, j: (i * nt + j, 0)
    bm = lambda i, j: (i, 0, 0)
    return pl.pallas_call(
        _xattn_kernel,
        grid=(n // seq_len, nt),
        in_specs=[pl.BlockSpec((tm, D_MODEL), row), pl.BlockSpec((1, MEM_TOKENS, X_WIDTH), bm),
                  pl.BlockSpec((1, MEM_TOKENS, X_WIDTH), bm), _full((1, D_MODEL)), _full((D_MODEL, X_WIDTH)),
                  _full((X_WIDTH, D_MODEL))],
        out_specs=pl.BlockSpec((tm, D_MODEL), row),
        out_shape=jax.ShapeDtypeStruct((n, D_MODEL), F32),
        compiler_params=_cparams(("parallel", "arbitrary"), 32),
        name="mem_xattn",
    )(h2d, mem_k, mem_v, pw["norm_x"], pw["xa_wq"], pw["xa_wo"])


def _router_kernel(h_ref, nf_ref, rw_ref, rb_ref, loc_ref, g_ref, cb_ref, nt_ref, cnt_ref, cnt_scr):
    tm = h_ref.shape[0]

    @pl.when(pl.program_id(0) == 0)
    def _():
        cnt_scr[...] = jnp.zeros_like(cnt_scr)

    xn = _rms(h_ref[...], nf_ref[...])
    x_hi, x_lo = _split2(xn)
    w_hi, w_lo = _split2(rw_ref[...])
    logits = _nt(w_hi, x_hi) + _nt(w_hi, x_lo) + _nt(w_lo, x_hi) + rb_ref[...]
    eid = lax.broadcasted_iota(I32, logits.shape, 0)
    vals, hots = [], []
    l = logits
    for k in range(TOP_K):
        m = jnp.max(l, axis=0, keepdims=True)
        idx = jnp.min(jnp.where(l == m, eid, N_EXPERTS), axis=0, keepdims=True)
        hot = eid == idx
        vals.append(m)
        hots.append(hot)
        l = jnp.where(hot, -jnp.inf, l)
    ex = [jnp.exp(vv - vals[0]) for vv in vals]
    den = ex[0] + ex[1] + ex[2] + ex[3]
    for k in range(TOP_K):
        g_ref[k:k + 1, :] = ex[k] / den
    hot_all = jnp.where(hots[0] | hots[1] | hots[2] | hots[3], 1.0, 0.0).astype(F32)
    ts = lax.broadcasted_iota(I32, (tm, tm), 0)
    tt = lax.broadcasted_iota(I32, (tm, tm), 1)
    upper = jnp.where(ts < tt, 1.0, 0.0).astype(BF16)
    before = _nn(hot_all.astype(BF16), upper)
    n_t = jnp.sum(hot_all, axis=1, keepdims=True)
    er = lax.broadcasted_iota(I32, (N_EXPERTS, N_EXPERTS), 0)
    ec = lax.broadcasted_iota(I32, (N_EXPERTS, N_EXPERTS), 1)
    lower = jnp.where(ec < er, 1.0, 0.0).astype(BF16)
    off = _nn(lower, jnp.broadcast_to(n_t, (N_EXPERTS, 128)).astype(BF16))[:, 0:1]
    pos = before + off
    for k in range(TOP_K):
        loc_ref[k:k + 1, :] = jnp.sum(jnp.where(hots[k], pos, 0.0), axis=0, keepdims=True).astype(I32)
    cb_ref[0] = jnp.broadcast_to(cnt_scr[...], (N_EXPERTS, 128)).astype(I32)
    nt_ref[0] = jnp.broadcast_to(n_t, (N_EXPERTS, 128)).astype(I32)
    cnt_scr[...] = cnt_scr[...] + n_t
    cnt_ref[...] = jnp.broadcast_to(cnt_scr[...], cnt_ref.shape).astype(I32)


def _router(h2d, pw, tm):
    n = h2d.shape[0]
    nt = n // tm
    col = lambda i: (0, i)
    per_tile = lambda i: (i, 0, 0)
    return pl.pallas_call(
        _router_kernel,
        grid=(nt,),
        in_specs=[pl.BlockSpec((tm, D_MODEL), lambda i: (i, 0)), _full((1, D_MODEL)), _full((N_EXPERTS, D_MODEL)),
                  _full((N_EXPERTS, 1))],
        out_specs=[pl.BlockSpec((TOP_K, tm), col), pl.BlockSpec((TOP_K, tm), col),
                   pl.BlockSpec((1, N_EXPERTS, 128), per_tile), pl.BlockSpec((1, N_EXPERTS, 128), per_tile),
                   _full((N_EXPERTS, 128))],
        out_shape=[jax.ShapeDtypeStruct((TOP_K, n), I32), jax.ShapeDtypeStruct((TOP_K, n), F32),
                   jax.ShapeDtypeStruct((nt, N_EXPERTS, 128), I32), jax.ShapeDtypeStruct((nt, N_EXPERTS, 128), I32),
                   jax.ShapeDtypeStruct((N_EXPERTS, 128), I32)],
        scratch_shapes=[pltpu.VMEM((N_EXPERTS, 1), F32)],
        compiler_params=_cparams(("arbitrary",), 32),
        name="moe_router",
    )(h2d, pw["norm_ffn"], pw["router_wt"], pw["router_b"])


RUN_CHUNK = 64


def _run_copies(n, src, dst, start_fn):
    nfull = lax.shift_right_logical(n, 6)

    def body(c, carry):
        start_fn(src + c * RUN_CHUNK, dst + c * RUN_CHUNK, RUN_CHUNK)
        return carry

    lax.fori_loop(0, nfull, body, 0)
    base = lax.shift_left(nfull, 6)
    low = n - base
    for b in (32, 16, 8, 4, 2, 1):
        @pl.when(jnp.bitwise_and(low, b) != 0)
        def _():
            o = base + jnp.bitwise_and(low, RUN_CHUNK - 2 * b)
            start_fn(src + o, dst + o, b)


def _dispatch_kernel(rt, pe_ref, pd_ref, h_ref, nf_ref, loc_ref, run_ref, xd_ref, xs3, zbuf, sems, zsem):
    tm = h_ref.shape[0]
    na = TOP_K * tm
    i = pl.program_id(0)
    slot = lax.rem(i, 2)

    @pl.when(i == 0)
    def _():
        zbuf[...] = jnp.zeros_like(zbuf)
        for e in range(N_EXPERTS):
            @pl.when(pd_ref[e] > 0)
            def _():
                st = pl.multiple_of((pe_ref[e] - rt) * 8, rt * 8)
                cp = pltpu.make_async_copy(zbuf, xd_ref.at[pl.ds(st, rt * 8)], zsem)
                cp.start()
                cp.wait()

    def drain(s):
        pltpu.make_async_copy(xs3.at[0], xd_ref.at[pl.ds(0, na * 8)], sems.at[s]).wait()

    xn = _rms(h_ref[...], nf_ref[...]).astype(BF16)
    rows = lax.broadcasted_iota(I32, (na, tm), 0)
    hit = rows == loc_ref[0:1, :]
    for k in range(1, TOP_K):
        hit = hit | (rows == loc_ref[k:k + 1, :])
    xs = _nn(jnp.where(hit, 1.0, 0.0).astype(BF16), xn)

    @pl.when(i > 1)
    def _():
        drain(slot)

    for j in range(D_MODEL // 128):
        xs3[slot, pl.ds(j, na, stride=8), :] = xs[:, 128 * j:128 * (j + 1)]

    def start(sr, ds_, ln):
        pltpu.make_async_copy(xs3.at[slot, pl.ds(pl.multiple_of(sr * 8, 8), ln * 8)],
                              xd_ref.at[pl.ds(pl.multiple_of(ds_ * 8, 8), ln * 8)], sems.at[slot]).start()

    for e in range(N_EXPERTS):
        _run_copies(run_ref[0, e], run_ref[1, e], run_ref[2, e], start)

    @pl.when(i == pl.num_programs(0) - 1)
    def _():
        drain(slot)

        @pl.when(i > 0)
        def _():
            drain(1 - slot)


def _dispatch(h2d, loc, runs, pad_end, padded, cap, rt, pw, tm):
    n = h2d.shape[0]
    gs = pltpu.PrefetchScalarGridSpec(
        num_scalar_prefetch=2,
        grid=(n // tm,),
        in_specs=[pl.BlockSpec((tm, D_MODEL), lambda i, *_: (i, 0)), pl.BlockSpec((1, D_MODEL), lambda i, *_: (0, 0)),
                  pl.BlockSpec((TOP_K, tm), lambda i, *_: (0, i)),
                  pl.BlockSpec((None, 3, N_EXPERTS), lambda i, *_: (i, 0, 0), memory_space=pltpu.SMEM)],
        out_specs=pl.BlockSpec(memory_space=pl.ANY),
        scratch_shapes=[pltpu.VMEM((2, TOP_K * tm * 8, 128), F32), pltpu.VMEM((rt * 8, 128), F32),
                        pltpu.SemaphoreType.DMA((2,)), pltpu.SemaphoreType.DMA],
    )
    return pl.pallas_call(
        functools.partial(_dispatch_kernel, rt),
        grid_spec=gs,
        out_shape=jax.ShapeDtypeStruct((cap * 8, 128), F32),
        compiler_params=_cparams(("arbitrary",), 40),
        name="moe_dispatch",
    )(pad_end, padded, h2d, pw["norm_ffn"], loc, runs)


def _expert_kernel(te_ref, nu_ref, x_ref, wg_ref, bg_ref, wu_ref, bu_ref, wd_ref, bd_ref, y_ref):
    @pl.when(pl.program_id(0) < nu_ref[0])
    def _():
        rt = y_ref.shape[0] // 8
        nj = D_MODEL // 128
        x = jnp.concatenate([x_ref[pl.ds(j, rt, stride=8), :] for j in range(nj)], axis=1).astype(BF16)
        gate = jnp.minimum(_nn(x, wg_ref[0]) + bg_ref[0], SWIGLU_LIMIT)
        up = jnp.clip(_nn(x, wu_ref[0]) + bu_ref[0], -SWIGLU_LIMIT, SWIGLU_LIMIT)
        hid = (up + 1.0) * (gate * _sigmoid(SWIGLU_ALPHA * gate))
        y = _nn(hid.astype(BF16), wd_ref[0]) + bd_ref[0]
        for j in range(nj):
            y_ref[pl.ds(j, rt, stride=8), :] = y[:, 128 * j:128 * (j + 1)]


def _experts(x_disp, tile_expert, n_used, pw, rt):
    cap = x_disp.shape[0] // 8
    d_ff = pw["w_gate"].shape[2]
    tile = lambda i, te, nu: (jnp.minimum(i, nu[0] - 1), 0)
    wsel = lambda i, te, nu: (te[jnp.minimum(i, nu[0] - 1)], 0, 0)
    gs = pltpu.PrefetchScalarGridSpec(
        num_scalar_prefetch=2,
        grid=(cap // rt,),
        in_specs=[pl.BlockSpec((rt * 8, 128), tile),
                  pl.BlockSpec((1, D_MODEL, d_ff), wsel), pl.BlockSpec((1, 1, d_ff), wsel),
                  pl.BlockSpec((1, D_MODEL, d_ff), wsel), pl.BlockSpec((1, 1, d_ff), wsel),
                  pl.BlockSpec((1, d_ff, D_MODEL), wsel), pl.BlockSpec((1, 1, D_MODEL), wsel)],
        out_specs=pl.BlockSpec((rt * 8, 128), tile),
    )
    return pl.pallas_call(
        _expert_kernel,
        grid_spec=gs,
        out_shape=jax.ShapeDtypeStruct((cap * 8, 128), F32),
        compiler_params=_cparams(("arbitrary",), 56),
        name="moe_experts",
    )(tile_expert, n_used, x_disp, pw["w_gate"], pw["b_gate"], pw["w_up"], pw["b_up"], pw["w_down"], pw["b_down"])


def _combine_kernel(h_ref, g_ref, loc_ref, nfin_ref, run_ref, runn_ref, yd_ref, o_ref, ys3, sems):
    tm = h_ref.shape[0]
    na = TOP_K * tm
    i = pl.program_id(0)
    slot = lax.rem(i, 2)

    def issue(rref, s):
        def start(sr, ds_, ln):
            pltpu.make_async_copy(yd_ref.at[pl.ds(pl.multiple_of(ds_ * 8, 8), ln * 8)],
                                  ys3.at[s, pl.ds(pl.multiple_of(sr * 8, 8), ln * 8)], sems.at[s]).start()
        for e in range(N_EXPERTS):
            _run_copies(rref[0, e], rref[1, e], rref[2, e], start)

    @pl.when(i == 0)
    def _():
        issue(run_ref, 0)

    @pl.when(i + 1 < pl.num_programs(0))
    def _():
        issue(runn_ref, 1 - slot)

    pltpu.make_async_copy(yd_ref.at[pl.ds(0, na * 8)], ys3.at[0], sems.at[slot]).wait()
    ys = jnp.concatenate([ys3[slot, pl.ds(j, na, stride=8), :] for j in range(D_MODEL // 128)], axis=1).astype(BF16)
    cols = lax.broadcasted_iota(I32, (tm, na), 1)
    g = g_ref[...]
    loc = loc_ref[...]
    gm = jnp.where(cols == loc[:, 0:1], g[:, 0:1], 0.0)
    for k in range(1, TOP_K):
        gm = gm + jnp.where(cols == loc[:, k:k + 1], g[:, k:k + 1], 0.0)
    o_ref[...] = _rms(h_ref[...] + _nn(gm.astype(BF16), ys), nfin_ref[...])


def _combine(h2d, gates_t, loc_t, runs, y_disp, pw, tm):
    n = h2d.shape[0]
    nt = n // tm
    row = lambda i: (i, 0)
    return pl.pallas_call(
        _combine_kernel,
        grid=(nt,),
        in_specs=[pl.BlockSpec((tm, D_MODEL), row), pl.BlockSpec((tm, TOP_K), row), pl.BlockSpec((tm, TOP_K), row),
                  _full((1, D_MODEL)),
                  pl.BlockSpec((None, 3, N_EXPERTS), lambda i: (i, 0, 0), memory_space=pltpu.SMEM),
                  pl.BlockSpec((None, 3, N_EXPERTS), lambda i: (jnp.minimum(i + 1, nt - 1), 0, 0),
                               memory_space=pltpu.SMEM),
                  pl.BlockSpec(memory_space=pl.ANY)],
        out_specs=pl.BlockSpec((tm, D_MODEL), row),
        out_shape=jax.ShapeDtypeStruct((n, D_MODEL), F32),
        scratch_shapes=[pltpu.VMEM((2, TOP_K * tm * 8, 128), F32), pltpu.SemaphoreType.DMA((2,))],
        compiler_params=_cparams(("arbitrary",), 40),
        name="moe_combine",
    )(h2d, gates_t, loc_t, pw["norm_final"], runs, runs, y_disp)


def _moe_and_final_norm(h2d, pw, rt, tm):
    n = h2d.shape[0]
    loc, gates, cnt_before, n_tile, counts = _router(h2d, pw, tm)
    counts = counts[:, 0]
    padded = (counts + rt - 1) // rt * rt
    pad_end = jnp.cumsum(padded).astype(I32)
    pad_start = pad_end - padded
    n_tiles = (n * TOP_K) // rt + N_EXPERTS
    cap = n_tiles * rt
    n_te = n_tile[:, :, 0]
    runs = jnp.stack([n_te, jnp.cumsum(n_te, axis=1) - n_te, pad_start[None, :] + cnt_before[:, :, 0]], axis=1).astype(I32)
    tile_expert = jnp.minimum(jnp.sum((pad_end[None, :] <= (jnp.arange(n_tiles, dtype=I32) * rt)[:, None]).astype(I32),
                                      axis=1), N_EXPERTS - 1).astype(I32)
    n_used = (pad_end[-1:] // rt).astype(I32)
    x_disp = _dispatch(h2d, loc, runs, pad_end, padded.astype(I32), cap, rt, pw, tm)
    y_disp = _experts(x_disp, tile_expert, n_used, pw, rt)
    return _combine(h2d, gates.T, loc.T, runs, y_disp, pw, tm)


def _prep_weights(norm_mix, w_in, mu_shift, rw_w0, rw_w2, rw_a0, rw_a2, rw_g2, rw_k_k, rw_k_a, rw_r_k, rw_lnx_w,
                  rw_lnx_b, mla_q_norm, mla_w_qb, mla_kv_norm, mla_w_kvb, mla_out_norm, w_out, norm_x, norm_mem,
                  xa_wq, xa_wk, xa_wv, xa_wo, norm_ffn, router_w, router_b, moe_w_gate, moe_b_gate, moe_w_up,
                  moe_b_up, moe_w_down, moe_b_down, norm_final):
    w_m = w_in[:, RW_PROJ:]
    qb = mla_w_qb.reshape(Q_RANK, MLA_HEADS, QK_NOPE + QK_ROPE)
    w_kv = mla_w_kvb.reshape(KV_RANK, MLA_HEADS, QK_NOPE + V_HEAD)
    z64 = jnp.zeros((64, RW_WIDTH), F32)
    blk = jnp.arange(RW_WIDTH) // RW_HEAD_DIM
    return {
        "norm_mix": norm_mix.reshape(1, -1),
        "w_r": w_in[:, :RW_PROJ].astype(BF16),
        "w_m": jnp.concatenate([w_m[:, :640], jnp.tile(w_m[:, 640:672], (1, 4)), jnp.tile(w_m[:, 672:704], (1, 4))],
                               axis=1).astype(BF16),
        "q_norm": mla_q_norm.reshape(1, -1),
        "w_qb": jnp.concatenate([qb[:, :, :QK_NOPE].reshape(Q_RANK, -1), qb[:, :, QK_NOPE:QK_NOPE + 32].reshape(Q_RANK, -1),
                                 qb[:, :, QK_NOPE + 32:].reshape(Q_RANK, -1)], axis=1).astype(BF16),
        "kv_norm": mla_kv_norm.reshape(1, -1),
        "wk": jnp.transpose(w_kv[:, :, :QK_NOPE], (1, 2, 0)).astype(BF16),
        "wv": jnp.transpose(w_kv[:, :, QK_NOPE:], (1, 0, 2)).astype(BF16),
        "out_norm": mla_out_norm.reshape(1, -1),
        "w_out": w_out.astype(BF16),
        "mu": mu_shift.reshape(1, -1),
        "rw_vec": jnp.stack([rw_w0, rw_a0, rw_k_k, rw_k_a, rw_r_k.reshape(-1), rw_lnx_w, rw_lnx_b,
                             jnp.zeros_like(rw_w0)], axis=0),
        "w2a2": jnp.stack([jnp.concatenate([rw_w2, z64], axis=0), jnp.concatenate([z64, rw_a2], axis=0)]).astype(BF16),
        "g2": rw_g2.astype(BF16),
        "ones64": (blk[:, None] == blk[None, :]).astype(BF16),
        "norm_x": norm_x.reshape(1, -1),
        "norm_mem": norm_mem.reshape(1, -1),
        "xa_wq": xa_wq.astype(BF16), "xa_wk": xa_wk.astype(BF16), "xa_wv": xa_wv.astype(BF16),
        "xa_wo": xa_wo.astype(BF16),
        "norm_ffn": norm_ffn.reshape(1, -1),
        "router_wt": router_w.T,
        "router_b": router_b.reshape(-1, 1),
        "w_gate": moe_w_gate.astype(BF16), "b_gate": moe_b_gate[:, None, :],
        "w_up": moe_w_up.astype(BF16), "b_up": moe_b_up[:, None, :],
        "w_down": moe_w_down.astype(BF16), "b_down": moe_b_down[:, None, :],
        "norm_final": norm_final.reshape(1, -1),
    }


def _pick(n, prefs):
    for p in prefs:
        if n % p == 0:
            return p
    return n


def kernel(x_prompt, x_sample, mem_prompt, cache_ckv, cache_krope, cache_mem_k, cache_mem_v, state_rwkv, state_shift, page_table, norm_mix, w_in, mu_shift, rw_w0, rw_w2, rw_a0, rw_a2, rw_g2, rw_k_k, rw_k_a, rw_r_k, rw_lnx_w, rw_lnx_b, mla_q_norm, mla_w_qb, mla_kv_norm, mla_w_kvb, mla_out_norm, w_out, norm_x, norm_mem, xa_wq, xa_wk, xa_wv, xa_wo, norm_ffn, router_w, router_b, moe_w_gate, moe_b_gate, moe_w_up, moe_b_up, moe_w_down, moe_b_down, norm_final):
    assert w_in.shape[0] == 1, "single-layer trunk"
    bp, tp, _ = x_prompt.shape
    bs, ts, _ = x_sample.shape
    n_pages = page_table.shape[1]
    past_len = n_pages * PAGE_SIZE
    assert ts <= RW_CHUNK and tp % RW_CHUNK == 0
    pw = _prep_weights(norm_mix[0], w_in[0], mu_shift[0], rw_w0[0], rw_w2[0], rw_a0[0], rw_a2[0], rw_g2[0],
                       rw_k_k[0], rw_k_a[0], rw_r_k[0], rw_lnx_w[0], rw_lnx_b[0], mla_q_norm[0], mla_w_qb[0],
                       mla_kv_norm[0], mla_w_kvb[0], mla_out_norm[0], w_out[0], norm_x[0], norm_mem[0], xa_wq[0],
                       xa_wk[0], xa_wv[0], xa_wo[0], norm_ffn[0], router_w[0], router_b[0], moe_w_gate[0],
                       moe_b_gate[0], moe_w_up[0], moe_b_up[0], moe_w_down[0], moe_b_down[0], norm_final)
    np_, ns_ = bp * tp, bs * ts
    rt = _pick(ns_ * TOP_K, (512, 256, 128))

    xp2 = x_prompt.reshape(np_, D_MODEL)
    projr_p, qcat_p, kcat_p, ckv_p, krope_p = _mix_in(xp2, tp, 0, pw, _pick(tp, (256, 128, 64)))
    yrw_p, rwkv_p = _rwkv(projr_p.reshape(bp, tp, RW_PROJ), jnp.zeros((bp, RW_PROJ), F32),
                          jnp.zeros((bp, RW_HEADS, 64, 64), F32), pw, RW_CHUNK, _pick(tp, (256, 128, 64)),
                          _pick(tp, (256, 128, 64)) // RW_CHUNK)
    tq = _pick(tp, (256, 128, 64))
    h1_p = _mla_prompt(qcat_p, kcat_p.reshape(bp, tp, KCAT), xp2, yrw_p.reshape(np_, RW_WIDTH), pw, tq,
                       _pick(tp, (512, 256, 128, 64)))
    mk2, mv2 = _mem_kv(mem_prompt.reshape(bp * MEM_TOKENS, D_MODEL), pw, _pick(bp * MEM_TOKENS, (512, 256)))
    h2_p = _xattn(h1_p, mk2.reshape(bp, MEM_TOKENS, X_WIDTH), mv2.reshape(bp, MEM_TOKENS, X_WIDTH), tp, pw,
                  _pick(tp, (512, 256, 128, 64)))
    y_p = _moe_and_final_norm(h2_p, pw, rt, _pick(np_, (256, 128)))

    xs2 = x_sample.reshape(ns_, D_MODEL)
    projr_s, qcat_s, _, ckv_s, krope_s = _mix_in(xs2, ts, past_len, pw, _pick(ns_, (256, 128, 64, 8)))
    projr_s3 = projr_s.reshape(bs, ts, RW_PROJ)
    projr_pad = jnp.pad(projr_s3, ((0, 0), (0, RW_CHUNK - ts), (0, 0)))
    yrw_s, rwkv_s = _rwkv(projr_pad, state_shift[0], state_rwkv[0], pw, ts, RW_CHUNK, 1, _pick(bs, (4, 2, 1)))
    yrw_s = yrw_s[:, :ts].reshape(ns_, RW_WIDTH)
    q4 = qcat_s.reshape(bs, ts, MLA_HEADS, KCAT).transpose(0, 2, 1, 3)
    q_lat = q4[..., :KV_RANK].reshape(bs, MLA_HEADS * ts, KV_RANK)
    o1 = q4[..., KV_RANK:KV_RANK + 128].reshape(bs, MLA_HEADS, ts, MLA_HEADS, 32)
    o2 = q4[..., KV_RANK + 128:].reshape(bs, MLA_HEADS, ts, MLA_HEADS, 32)
    q_rope = jnp.stack([jnp.concatenate([o1[:, h, :, h], o2[:, h, :, h]], axis=-1) for h in range(MLA_HEADS)],
                       axis=1).reshape(bs, MLA_HEADS * ts, QK_ROPE)
    new_ckv = jnp.pad(ckv_s.reshape(bs, ts, KV_RANK), ((0, 0), (0, PAGE_SIZE - ts), (0, 0)))
    new_kr = jnp.pad(krope_s.reshape(bs, ts, QK_ROPE), ((0, 0), (0, PAGE_SIZE - ts), (0, 0)))
    o_s = _paged_attn(q_lat, q_rope, new_ckv, new_kr, cache_ckv, jnp.swapaxes(cache_krope, 2, 3), page_table, ts,
                      _pick(n_pages, (16, 8, 4, 2, 1)))
    o_s = o_s.reshape(bs, MLA_HEADS, ts, KV_RANK).transpose(0, 2, 1, 3).reshape(ns_, MLA_HEADS * KV_RANK)
    h1_s = _out_proj_call(o_s, xs2, yrw_s, pw, _pick(ns_, (256, 128, 64, 8)))
    h2_s = _xattn(h1_s, cache_mem_k[0].reshape(bs, MEM_TOKENS, X_WIDTH), cache_mem_v[0].reshape(bs, MEM_TOKENS, X_WIDTH),
                  ts, pw, ts)
    y_s = _moe_and_final_norm(h2_s, pw, rt, _pick(ns_, (256, 128)))

    return (y_p.reshape(bp, tp, D_MODEL), y_s.reshape(bs, ts, D_MODEL),
            ckv_p.reshape(1, bp, tp, KV_RANK), krope_p.reshape(1, bp, tp, QK_ROPE),
            mk2.reshape(1, bp, MEM_TOKENS, X_HEADS, X_HEAD_DIM), mv2.reshape(1, bp, MEM_TOKENS, X_HEADS, X_HEAD_DIM),
            rwkv_p[None], projr_p.reshape(bp, tp, RW_PROJ)[None, :, -1],
            ckv_s.reshape(1, bs, ts, KV_RANK), krope_s.reshape(1, bs, ts, QK_ROPE),
            rwkv_s[None], projr_s3[None, :, -1])
```

```python
import functools

import jax
import jax.numpy as jnp
from jax import lax
from jax.experimental import pallas as pl
from jax.experimental.pallas import tpu as pltpu

F32 = jnp.float32
BF16 = jnp.bfloat16
I32 = jnp.int32

D_MODEL = 1024
PAGE_SIZE = 128
RW_HEADS = 8
RW_HEAD_DIM = 64
RW_WIDTH = 512
RW_PROJ = 1792
GN_EPS = 64e-5
L2_EPS = 1e-12
MLA_HEADS = 4
QK_NOPE = 128
QK_ROPE = 64
V_HEAD = 128
Q_RANK = 384
KV_RANK = 256
ROPE_THETA = 10000.0
MEM_TOKENS = 256
X_HEADS = 4
X_HEAD_DIM = 128
X_WIDTH = 512
N_EXPERTS = 32
TOP_K = 4
SWIGLU_LIMIT = 7.0
SWIGLU_ALPHA = 1.702
NORM_EPS = 1e-5
MLA_SCALE = (QK_NOPE + QK_ROPE) ** -0.5
X_SCALE = X_HEAD_DIM ** -0.5
NEG_BIG = -1e30

KCAT = KV_RANK + 2 * 128
RW_CHUNK = 64
VMEM_LIMIT_V7X = 56 * 1024 * 1024


def _cparams(sem, vmem_mib=None):
    kw = dict(dimension_semantics=sem)
    if vmem_mib is not None:
        kw["vmem_limit_bytes"] = min(vmem_mib * 1024 * 1024, VMEM_LIMIT_V7X)
    return pltpu.CompilerParams(**kw)


def _nn(a, b):
    return jnp.dot(a, b, preferred_element_type=F32)


def _nt(a, b):
    return lax.dot_general(a, b, (((1,), (1,)), ((), ())), preferred_element_type=F32)


def _tn(a, b):
    return lax.dot_general(a, b, (((0,), (0,)), ((), ())), preferred_element_type=F32)


def _split2(x):
    hi = x.astype(BF16)
    lo = (x - hi.astype(F32)).astype(BF16)
    return hi, lo


def _split3(x):
    p1 = x.astype(BF16)
    r1 = x - p1.astype(F32)
    p2 = r1.astype(BF16)
    p3 = (r1 - p2.astype(F32)).astype(BF16)
    return p1, p2, p3


def _rms(x, g, eps=NORM_EPS):
    return x * lax.rsqrt(jnp.mean(x * x, axis=-1, keepdims=True) + eps) * g


def _sigmoid(x):
    return 1.0 / (1.0 + jnp.exp(-x))


def _full(shape):
    n = len(shape)
    return pl.BlockSpec(shape, lambda *a: (0,) * n)


def _mix_in_kernel(x_ref, nm_ref, wr_ref, wm_ref, qn_ref, wqb_ref, kvn_ref, wk_ref, c4_ref, s4_ref,
                   projr_ref, qcat_ref, kcat_ref, ckv_ref, krope_ref):
    xn = _rms(x_ref[...], nm_ref[...]).astype(BF16)
    projr_ref[...] = _nn(xn, wr_ref[...])
    pm = _nn(xn, wm_ref[...])
    q_a = pm[:, :Q_RANK]
    lat = pm[:, Q_RANK:Q_RANK + KV_RANK]
    k1 = pm[:, 640:768]
    k2 = pm[:, 768:896]
    c4 = c4_ref[...]
    s4 = s4_ref[...]
    ckv = _rms(lat, kvn_ref[...])
    ckv_ref[...] = ckv
    ok1 = k1 * c4 - k2 * s4
    ok2 = k1 * s4 + k2 * c4
    krope_ref[...] = jnp.concatenate([ok1[:, :32], ok2[:, :32]], axis=1)
    kcat_ref[...] = jnp.concatenate([ckv, ok1, ok2], axis=1).astype(BF16)
    qn = _rms(q_a, qn_ref[...]).astype(BF16)
    q = _nn(qn, wqb_ref[...]) * MLA_SCALE
    r1 = q[:, 512:640]
    r2 = q[:, 640:768]
    o1 = r1 * c4 - r2 * s4
    o2 = r1 * s4 + r2 * c4
    lane = lax.broadcasted_iota(I32, o1.shape, 1)
    for h in range(MLA_HEADS):
        ql = _nn(q[:, 128 * h:128 * h + 128].astype(BF16), wk_ref[h])
        mh = (lane >= 32 * h) & (lane < 32 * h + 32)
        qcat_ref[:, KCAT * h:KCAT * (h + 1)] = jnp.concatenate(
            [ql, jnp.where(mh, o1, 0.0), jnp.where(mh, o2, 0.0)], axis=1).astype(BF16)


def _mix_in(x2d, seq_len, pos0, pw, tm):
    n = x2d.shape[0]
    half = QK_ROPE // 2
    inv = ROPE_THETA ** (-jnp.arange(half, dtype=F32) / half)
    pos = (pos0 + jnp.arange(seq_len, dtype=jnp.int32)).astype(F32)
    ang = pos[:, None] * inv[None, :]
    tab_len = max(seq_len, tm)
    c4 = jnp.tile(jnp.cos(ang), (tab_len // seq_len, 4))
    s4 = jnp.tile(jnp.sin(ang), (tab_len // seq_len, 4))
    ntab = tab_len // tm
    row = lambda i: (i, 0)
    tab = lambda i: (i % ntab, 0)
    outs = pl.pallas_call(
        _mix_in_kernel,
        grid=(n // tm,),
        in_specs=[pl.BlockSpec((tm, D_MODEL), row), _full((1, D_MODEL)), _full((D_MODEL, RW_PROJ)),
                  _full((D_MODEL, 896)), _full((1, Q_RANK)), _full((Q_RANK, 768)), _full((1, KV_RANK)),
                  _full((MLA_HEADS, QK_NOPE, KV_RANK)), pl.BlockSpec((tm, 128), tab), pl.BlockSpec((tm, 128), tab)],
        out_specs=[pl.BlockSpec((tm, RW_PROJ), row), pl.BlockSpec((tm, MLA_HEADS * KCAT), row),
                   pl.BlockSpec((tm, KCAT), row), pl.BlockSpec((tm, KV_RANK), row), pl.BlockSpec((tm, QK_ROPE), row)],
        out_shape=[jax.ShapeDtypeStruct((n, RW_PROJ), F32), jax.ShapeDtypeStruct((n, MLA_HEADS * KCAT), BF16),
                   jax.ShapeDtypeStruct((n, KCAT), BF16), jax.ShapeDtypeStruct((n, KV_RANK), F32),
                   jax.ShapeDtypeStruct((n, QK_ROPE), F32)],
        compiler_params=_cparams(("parallel",), 48),
        name="mix_in",
    )(x2d, pw["norm_mix"], pw["w_r"], pw["w_m"], pw["q_norm"], pw["w_qb"], pw["kv_norm"], pw["wk"], c4, s4)
    return outs


def _rwkv_kernel(t_valid, ng, nb, proj_ref, shift_ref, st0_ref, mu_ref, vec_ref, w2a2_ref, g2_ref, ones_ref,
                 y_ref, st_ref, s_scr, carry_scr):
    C = RW_CHUNK
    lc = C.bit_length() - 1
    nseg = ng * nb
    gr = nseg * C
    tb = proj_ref.shape[1]
    npair = RW_HEADS // 2

    if nb == 1:
        @pl.when(pl.program_id(1) == 0)
        def _():
            s_scr[...] = st0_ref[0]
            carry_scr[...] = shift_ref[0]

    mu = mu_ref[...]
    w0 = vec_ref[0:1, :]
    a0 = vec_ref[1:2, :]
    k_k = vec_ref[2:3, :]
    k_a = vec_ref[3:4, :]
    r_k = vec_ref[4:5, :]
    lnx_w = vec_ref[5:6, :]
    lnx_b = vec_ref[6:7, :]
    ones = ones_ref[...]

    def bsum(x):
        return _nn(x.astype(BF16), ones)

    row = lax.broadcasted_iota(I32, (gr, 1), 0)
    lane128 = lax.broadcasted_iota(I32, (C, 128), 1)
    m0 = lane128 < RW_HEAD_DIM
    rr = lax.broadcasted_iota(I32, (2 * C, 2 * C), 0)
    cc = lax.broadcasted_iota(I32, (2 * C, 2 * C), 1)
    strict = cc < rr
    incl = cc <= rr
    eye = jnp.where(cc == rr, 1.0, 0.0).astype(F32)
    tr = lax.broadcasted_iota(I32, (gr, gr), 0)
    tc = lax.broadcasted_iota(I32, (gr, gr), 1)
    same = jnp.right_shift(tr, lc) == jnp.right_shift(tc, lc)
    tri = jnp.where(same & (tc <= tr), 1.0, 0.0).astype(BF16)
    allc = jnp.where(same, 1.0, 0.0).astype(BF16)

    def stack(x):
        return jnp.concatenate([jnp.where(m0, x, 0.0), jnp.where(m0, 0.0, x)], axis=0).astype(BF16)

    items = [(c, p) for c in range(nseg) for p in range(npair)]

    def group(gi, carry):
        r0 = pl.multiple_of(gi * gr, gr)
        if nb == 1:
            x = proj_ref[0, pl.ds(r0, gr), :]
            prev = jnp.where(row == 0, carry_scr[...], pltpu.roll(x, 1, axis=0))
            carry_scr[...] = x[gr - 1:gr, :]
        else:
            x = proj_ref[...].reshape(gr, RW_PROJ)
            prev = pltpu.roll(x, 1, axis=0)
            for bi in range(nb):
                prev = jnp.where(row == bi * C, shift_ref[bi], prev)
        mixed = x + (prev - x) * mu
        r = mixed[:, 0:512]
        k = mixed[:, 512:1024]
        v = mixed[:, 1024:1536]
        wa = mixed[:, 1536:1664]
        gd = mixed[:, 1664:1792]
        zw = w0 + _nn(jnp.tanh(wa).astype(BF16), w2a2_ref[0])
        sp = jnp.maximum(-zw, 0.0) + jnp.log(1.0 + jnp.exp(-jnp.abs(zw)))
        logw = -jnp.exp(-sp - 0.5)
        a_lr = _sigmoid(a0 + _nn(wa.astype(BF16), w2a2_ref[1]))
        g = _nn(_sigmoid(gd).astype(BF16), g2_ref[...])
        kk = k * k_k
        kk = kk * lax.rsqrt(bsum(kk * kk) + L2_EPS)
        k_mod = k * (1.0 + (a_lr - 1.0) * k_a)
        a_v = -kk
        b_v = kk * a_lr
        if t_valid < C:
            valid = jnp.bitwise_and(row, C - 1) < t_valid
            logw = jnp.where(valid, logw, 0.0)
            a_v = jnp.where(valid, a_v, 0.0)
            b_v = jnp.where(valid, b_v, 0.0)
            k_mod = jnp.where(valid, k_mod, 0.0)
            v = jnp.where(valid, v, 0.0)
        p1, p2, p3 = _split3(logw)
        cum = _nn(tri, p1) + _nn(tri, p2) + _nn(tri, p3)
        tot = jnp.concatenate([jnp.broadcast_to(cum[c * C + C - 1:c * C + C, :], (C, RW_WIDTH)) for c in range(nseg)],
                              axis=0)
        e_neg = jnp.exp(-cum)
        e_rem = jnp.exp(tot - cum)
        r_t = r * jnp.exp(cum)
        a_t = a_v * jnp.exp(cum - logw)
        b_t = b_v * e_neg
        k_t = k_mod * e_neg
        b_g = b_v * e_rem
        k_g = k_mod * e_rem
        gam = jnp.exp(tot)

        def blk(arr, c, p):
            return stack(arr[c * C:(c + 1) * C, 128 * p:128 * p + 128])

        a2 = [blk(a_t, c, p) for c, p in items]
        r2 = [blk(r_t, c, p) for c, p in items]
        v2 = [blk(v, c, p) for c, p in items]
        gram = [_nt(jnp.concatenate([a2[i], r2[i]], axis=0),
                    jnp.concatenate([blk(b_t, c, p), blk(k_t, c, p)], axis=0)) for i, (c, p) in enumerate(items)]
        tinv = [eye + jnp.where(strict, gm[:2 * C, :2 * C], 0.0) for gm in gram]
        nab = [jnp.where(strict, gm[:2 * C, :2 * C], 0.0).astype(BF16) for gm in gram]
        akrk = [jnp.concatenate([jnp.where(strict, gm[:2 * C, 2 * C:], 0.0), jnp.where(incl, gm[2 * C:, 2 * C:], 0.0)],
                                axis=0).astype(BF16) for gm in gram]
        arb = [jnp.where(incl, gm[2 * C:, :2 * C], 0.0).astype(BF16) for gm in gram]
        pw = [_nn(n, n) for n in nab]
        for it in range(1, lc):
            pb = [q.astype(BF16) for q in pw]
            if it < lc - 1:
                res = [_nn(pb[i], jnp.concatenate([tinv[i].astype(BF16), pb[i]], axis=1)) for i in range(len(items))]
                tinv = [tinv[i] + res[i][:, :2 * C] for i in range(len(items))]
                pw = [q[:, 2 * C:] for q in res]
            else:
                tinv = [tinv[i] + _nn(pb[i], tinv[i].astype(BF16)) for i in range(len(items))]
        av = [_nn(akrk[i], v2[i]) for i in range(len(items))]
        tw = [_nn(tinv[i].astype(BF16), jnp.concatenate([a2[i], av[i][:2 * C].astype(BF16)], axis=1))
              for i in range(len(items))]

        yrows = []
        for c in range(nseg):
            idx = [c * npair + p for p in range(npair)]
            s_old = [s_scr[p] if nb == 1 else st0_ref[c, p] for p in range(npair)]
            sb = [s.astype(BF16) for s in s_old]
            u = [_nt(tw[i][:, :2 * C].astype(BF16), sb[p]) + tw[i][:, 2 * C:] for p, i in enumerate(idx)]
            ub = [q.astype(BF16) for q in u]
            y2 = [_nt(r2[i], sb[p]) + _nn(arb[i], ub[p]) + av[i][2 * C:] for p, i in enumerate(idx)]
            for p, i in enumerate(idx):
                s_new = s_old[p] * gam[c * C:c * C + 1, 128 * p:128 * p + 128] + _tn(
                    jnp.concatenate([ub[p], v2[i]], axis=0),
                    jnp.concatenate([blk(b_g, c, p), blk(k_g, c, p)], axis=0))
                if nb == 1:
                    s_scr[p] = s_new
                else:
                    st_ref[c, p] = s_new
            yrows.append(jnp.concatenate([q[:C] + q[C:] for q in y2], axis=1))
        y = yrows[0] if nseg == 1 else jnp.concatenate(yrows, axis=0)
        d = y - bsum(y) * (1.0 / RW_HEAD_DIM)
        var = bsum(d * d) * (1.0 / RW_HEAD_DIM)
        yn = d * lax.rsqrt(var + GN_EPS) * lnx_w + lnx_b
        bonus = bsum(r * k_mod * r_k) * v
        out = ((yn + bonus) * g).astype(BF16)
        if nb == 1:
            y_ref[0, pl.ds(r0, gr), :] = out
        else:
            y_ref[...] = out.reshape(nb, C, RW_WIDTH)
        return carry

    if nb == 1:
        lax.fori_loop(0, tb // gr, group, 0)

        @pl.when(pl.program_id(1) == pl.num_programs(1) - 1)
        def _():
            st_ref[0] = s_scr[...]
    else:
        group(0, 0)


def _pair_state(s):
    b = s.shape[0]
    s = s.reshape(b, 4, 2, 64, 64)
    z = jnp.zeros_like(s[:, :, 0])
    top = jnp.concatenate([s[:, :, 0], z], axis=-1)
    bot = jnp.concatenate([z, s[:, :, 1]], axis=-1)
    return jnp.concatenate([top, bot], axis=-2)


def _unpair_state(s2):
    b = s2.shape[0]
    return jnp.stack([s2[:, :, :64, :64], s2[:, :, 64:, 64:]], axis=2).reshape(b, RW_HEADS, 64, 64)


def _rwkv(proj_r3, shift_prev, state0, pw, t_valid, tb, ng, nb=1):
    b, t, _ = proj_r3.shape
    assert nb == 1 or (ng == 1 and tb == t == RW_CHUNK and b % nb == 0)
    y, st = pl.pallas_call(
        functools.partial(_rwkv_kernel, t_valid, ng, nb),
        grid=(b // nb, t // tb),
        in_specs=[pl.BlockSpec((nb, tb, RW_PROJ), lambda i, j: (i, j, 0)),
                  pl.BlockSpec((nb, 1, RW_PROJ), lambda i, j: (i, 0, 0)),
                  pl.BlockSpec((nb, 4, 128, 128), lambda i, j: (i, 0, 0, 0)),
                  _full((1, RW_PROJ)), _full((8, RW_WIDTH)), _full((2, 128, RW_WIDTH)), _full((128, RW_WIDTH)),
                  _full((RW_WIDTH, RW_WIDTH))],
        out_specs=[pl.BlockSpec((nb, tb, RW_WIDTH), lambda i, j: (i, j, 0)),
                   pl.BlockSpec((nb, 4, 128, 128), lambda i, j: (i, 0, 0, 0))],
        out_shape=[jax.ShapeDtypeStruct((b, t, RW_WIDTH), BF16), jax.ShapeDtypeStruct((b, 4, 128, 128), F32)],
        scratch_shapes=[pltpu.VMEM((4, 128, 128), F32), pltpu.VMEM((1, RW_PROJ), F32)],
        compiler_params=_cparams(("parallel", "arbitrary"), 40),
        name="rwkv7",
    )(proj_r3, shift_prev.reshape(b, 1, RW_PROJ), _pair_state(state0), pw["mu"], pw["rw_vec"], pw["w2a2"], pw["g2"],
      pw["ones64"])
    return y, _unpair_state(st)


def _out_proj(o_heads, x, yrw, wv_ref, on_ref, wo_ref):
    ys = [_nn(o_heads[h].astype(BF16), wv_ref[h]) for h in range(MLA_HEADS)]
    y_mla = _rms(jnp.concatenate(ys, axis=1), on_ref[...]).astype(BF16)
    return x + _nn(yrw, wo_ref[0:RW_WIDTH, :]) + _nn(y_mla, wo_ref[RW_WIDTH:, :])


def _mla_prompt_kernel(tk, q_ref, k_ref, x_ref, yrw_ref, wv_ref, on_ref, wo_ref, o_ref, acc_scr, m_scr, l_scr):
    tq = q_ref.shape[0]
    i = pl.program_id(1)
    acc_scr[...] = jnp.zeros_like(acc_scr)
    m_scr[...] = jnp.full_like(m_scr, NEG_BIG)
    l_scr[...] = jnp.zeros_like(l_scr)
    qpos = i * tq + lax.broadcasted_iota(I32, (tq, tk), 0)
    kidx = lax.broadcasted_iota(I32, (tq, tk), 1)
    n_kv = (i * tq + tq + tk - 1) // tk

    def body(j, carry):
        k0 = pl.multiple_of(j * tk, tk)
        kc = k_ref[0, pl.ds(k0, tk), :]
        vc = kc[:, :KV_RANK]
        mask = (kidx + j * tk) <= qpos

        def qk(h):
            return _nt(q_ref[:, KCAT * h:KCAT * (h + 1)], kc)

        def soft_pv(h, s):
            s = jnp.where(mask, s, NEG_BIG)
            m_old = m_scr[h]
            m_new = jnp.maximum(m_old, jnp.max(s, axis=1, keepdims=True))
            p = jnp.exp(s - m_new)
            corr = jnp.exp(m_old - m_new)
            l_scr[h] = l_scr[h] * corr + jnp.sum(p, axis=1, keepdims=True)
            m_scr[h] = m_new
            acc_scr[h] = acc_scr[h] * corr + _nn(p.astype(BF16), vc)

        s_prev = qk(0)
        for h in range(1, MLA_HEADS):
            s_next = qk(h)
            soft_pv(h - 1, s_prev)
            s_prev = s_next
        soft_pv(MLA_HEADS - 1, s_prev)
        return carry

    lax.fori_loop(0, n_kv, body, 0)
    o_heads = [acc_scr[h] / l_scr[h] for h in range(MLA_HEADS)]
    o_ref[...] = _out_proj(o_heads, x_ref[...], yrw_ref[...], wv_ref, on_ref, wo_ref)


def _mla_prompt(qcat, kcat3, x2d, yrw2d, pw, tq, tk):
    b, t, _ = kcat3.shape
    nq = t // tq
    row = lambda i, j: (i * nq + j, 0)
    return pl.pallas_call(
        functools.partial(_mla_prompt_kernel, tk),
        grid=(b, nq),
        in_specs=[pl.BlockSpec((tq, MLA_HEADS * KCAT), row), pl.BlockSpec((1, t, KCAT), lambda i, j: (i, 0, 0)),
                  pl.BlockSpec((tq, D_MODEL), row), pl.BlockSpec((tq, RW_WIDTH), row),
                  _full((MLA_HEADS, KV_RANK, V_HEAD)), _full((1, 512)), _full((D_MODEL, D_MODEL))],
        out_specs=pl.BlockSpec((tq, D_MODEL), row),
        out_shape=jax.ShapeDtypeStruct((b * t, D_MODEL), F32),
        scratch_shapes=[pltpu.VMEM((MLA_HEADS, tq, KV_RANK), F32), pltpu.VMEM((MLA_HEADS, tq, 1), F32),
                        pltpu.VMEM((MLA_HEADS, tq, 1), F32)],
        compiler_params=_cparams(("parallel", "arbitrary"), 40),
        name="mla_prompt",
    )(qcat, kcat3, x2d, yrw2d, pw["wv"], pw["out_norm"], pw["w_out"])


def _out_proj_kernel(o_ref, x_ref, yrw_ref, wv_ref, on_ref, wo_ref, out_ref):
    o_heads = [o_ref[:, KV_RANK * h:KV_RANK * (h + 1)] for h in range(MLA_HEADS)]
    out_ref[...] = _out_proj(o_heads, x_ref[...], yrw_ref[...], wv_ref, on_ref, wo_ref)


def _out_proj_call(o_lat, x2d, yrw2d, pw, tm):
    n = x2d.shape[0]
    row = lambda i: (i, 0)
    return pl.pallas_call(
        _out_proj_kernel,
        grid=(n // tm,),
        in_specs=[pl.BlockSpec((tm, MLA_HEADS * KV_RANK), row), pl.BlockSpec((tm, D_MODEL), row),
                  pl.BlockSpec((tm, RW_WIDTH), row), _full((MLA_HEADS, KV_RANK, V_HEAD)), _full((1, 512)),
                  _full((D_MODEL, D_MODEL))],
        out_specs=pl.BlockSpec((tm, D_MODEL), row),
        out_shape=jax.ShapeDtypeStruct((n, D_MODEL), F32),
        compiler_params=_cparams(("parallel",), 32),
        name="mla_out_proj",
    )(o_lat, x2d, yrw2d, pw["wv"], pw["out_norm"], pw["w_out"])


def _paged_attn_kernel(pg, t_new, pt_ref, ql_ref, qr_ref, nck_ref, nkr_ref, *rest):
    ck_refs = rest[:pg]
    kr_refs = rest[pg:2 * pg]
    o_ref = rest[2 * pg]
    m_scr, l_scr, acc_scr, kall, krall = rest[2 * pg + 1:]
    g = pl.program_id(1)

    @pl.when(g == 0)
    def _():
        m_scr[...] = jnp.full_like(m_scr, NEG_BIG)
        l_scr[...] = jnp.zeros_like(l_scr)
        acc_scr[...] = jnp.zeros_like(acc_scr)

    ql = ql_ref[0]
    qr = qr_ref[0]

    def merge(s, v):
        m_old = m_scr[...]
        m_new = jnp.maximum(m_old, jnp.max(s, axis=1, keepdims=True))
        p = jnp.exp(s - m_new)
        corr = jnp.exp(m_old - m_new)
        l_scr[...] = l_scr[...] * corr + jnp.sum(p, axis=1, keepdims=True)
        acc_scr[...] = acc_scr[...] * corr + _nn(p.astype(BF16), v)
        m_scr[...] = m_new

    for j in range(pg):
        kall[PAGE_SIZE * j:PAGE_SIZE * (j + 1), :] = ck_refs[j][...].astype(BF16)
        krall[:, PAGE_SIZE * j:PAGE_SIZE * (j + 1)] = kr_refs[j][...].astype(BF16)
    ka = kall[...]
    merge(_nt(ql, ka) + _nn(qr, krall[...]), ka)

    @pl.when(g == pl.num_programs(1) - 1)
    def _():
        nck = nck_ref[0].astype(BF16)
        s = _nt(ql, nck) + _nt(qr, nkr_ref[0].astype(BF16))
        rows = lax.broadcasted_iota(I32, s.shape, 0)
        cols = lax.broadcasted_iota(I32, s.shape, 1)
        s = jnp.where(cols <= rows % t_new, s, NEG_BIG)
        merge(s, nck)
        o_ref[0] = acc_scr[...] / l_scr[...]


def _paged_attn(q_lat, q_rope, new_ckv, new_kr, cache_ckv, cache_krope_t, page_table, t_new, pg):
    b, n_pages = page_table.shape
    nq = q_lat.shape[1]
    ck_specs = [pl.BlockSpec((None, None, PAGE_SIZE, KV_RANK),
                             (lambda i, g, pt, j=j: (0, pt[i * n_pages + g * pg + j], 0, 0))) for j in range(pg)]
    kr_specs = [pl.BlockSpec((None, None, QK_ROPE, PAGE_SIZE),
                             (lambda i, g, pt, j=j: (0, pt[i * n_pages + g * pg + j], 0, 0))) for j in range(pg)]
    bmap = lambda i, g, pt: (i, 0, 0)
    gs = pltpu.PrefetchScalarGridSpec(
        num_scalar_prefetch=1,
        grid=(b, n_pages // pg),
        in_specs=[pl.BlockSpec((1, nq, KV_RANK), bmap), pl.BlockSpec((1, nq, QK_ROPE), bmap),
                  pl.BlockSpec((1, PAGE_SIZE, KV_RANK), bmap), pl.BlockSpec((1, PAGE_SIZE, QK_ROPE), bmap)]
        + ck_specs + kr_specs,
        out_specs=pl.BlockSpec((1, nq, KV_RANK), bmap),
        scratch_shapes=[pltpu.VMEM((nq, 1), F32), pltpu.VMEM((nq, 1), F32), pltpu.VMEM((nq, KV_RANK), F32),
                        pltpu.VMEM((pg * PAGE_SIZE, KV_RANK), BF16), pltpu.VMEM((QK_ROPE, pg * PAGE_SIZE), BF16)],
    )
    return pl.pallas_call(
        functools.partial(_paged_attn_kernel, pg, t_new),
        grid_spec=gs,
        out_shape=jax.ShapeDtypeStruct((b, nq, KV_RANK), F32),
        compiler_params=_cparams(("parallel", "arbitrary"), 48),
        name="mla_paged",
    )(page_table.reshape(-1), q_lat, q_rope, new_ckv, new_kr, *([cache_ckv] * pg), *([cache_krope_t] * pg))


def _mem_kv_kernel(m_ref, nm_ref, wk_ref, wv_ref, k_ref, v_ref):
    mn = _rms(m_ref[...], nm_ref[...]).astype(BF16)
    k_ref[...] = _nn(mn, wk_ref[...])
    v_ref[...] = _nn(mn, wv_ref[...])


def _mem_kv(mem2d, pw, tm):
    n = mem2d.shape[0]
    row = lambda i: (i, 0)
    return pl.pallas_call(
        _mem_kv_kernel,
        grid=(n // tm,),
        in_specs=[pl.BlockSpec((tm, D_MODEL), row), _full((1, D_MODEL)), _full((D_MODEL, X_WIDTH)),
                  _full((D_MODEL, X_WIDTH))],
        out_specs=[pl.BlockSpec((tm, X_WIDTH), row), pl.BlockSpec((tm, X_WIDTH), row)],
        out_shape=[jax.ShapeDtypeStruct((n, X_WIDTH), F32)] * 2,
        compiler_params=_cparams(("parallel",), 32),
        name="mem_kv",
    )(mem2d, pw["norm_mem"], pw["xa_wk"], pw["xa_wv"])


def _xattn_kernel(h_ref, mk_ref, mv_ref, nx_ref, wq_ref, wo_ref, o_ref):
    h = h_ref[...]
    xn = _rms(h, nx_ref[...]).astype(BF16)
    q = (_nn(xn, wq_ref[...]) * X_SCALE).astype(BF16)
    mk = mk_ref[0].astype(BF16)
    mv = mv_ref[0].astype(BF16)
    outs = []
    for hh in range(X_HEADS):
        sl = slice(X_HEAD_DIM * hh, X_HEAD_DIM * (hh + 1))
        s = _nt(q[:, sl], mk[:, sl])
        p = jnp.exp(s - jnp.max(s, axis=1, keepdims=True))
        outs.append(_nn(p.astype(BF16), mv[:, sl]) / jnp.sum(p, axis=1, keepdims=True))
    o_ref[...] = h + _nn(jnp.concatenate(outs, axis=1).astype(BF16), wo_ref[...])


def _xattn(h2d, mem_k, mem_v, seq_len, pw, tm):
    n = h2d.shape[0]
    nt = seq_len // tm
    row = lambda i, j: (i * nt + j, 0)
    bm = lambda i, j: (i, 0, 0)
    return pl.pallas_call(
        _xattn_kernel,
        grid=(n // seq_len, nt),
        in_specs=[pl.BlockSpec((tm, D_MODEL), row), pl.BlockSpec((1, MEM_TOKENS, X_WIDTH), bm),
                  pl.BlockSpec((1, MEM_TOKENS, X_WIDTH), bm), _full((1, D_MODEL)), _full((D_MODEL, X_WIDTH)),
                  _full((X_WIDTH, D_MODEL))],
        out_specs=pl.BlockSpec((tm, D_MODEL), row),
        out_shape=jax.ShapeDtypeStruct((n, D_MODEL), F32),
        compiler_params=_cparams(("parallel", "arbitrary"), 32),
        name="mem_xattn",
    )(h2d, mem_k, mem_v, pw["norm_x"], pw["xa_wq"], pw["xa_wo"])


def _router_kernel(h_ref, nf_ref, rw_ref, rb_ref, loc_ref, g_ref, cb_ref, nt_ref, cnt_ref, cnt_scr):
    tm = h_ref.shape[0]

    @pl.when(pl.program_id(0) == 0)
    def _():
        cnt_scr[...] = jnp.zeros_like(cnt_scr)

    xn = _rms(h_ref[...], nf_ref[...])
    x_hi, x_lo = _split2(xn)
    w_hi, w_lo = _split2(rw_ref[...])
    logits = _nt(w_hi, x_hi) + _nt(w_hi, x_lo) + _nt(w_lo, x_hi) + rb_ref[...]
    eid = lax.broadcasted_iota(I32, logits.shape, 0)
    vals, hots = [], []
    l = logits
    for k in range(TOP_K):
        m = jnp.max(l, axis=0, keepdims=True)
        idx = jnp.min(jnp.where(l == m, eid, N_EXPERTS), axis=0, keepdims=True)
        hot = eid == idx
        vals.append(m)
        hots.append(hot)
        l = jnp.where(hot, -jnp.inf, l)
    ex = [jnp.exp(vv - vals[0]) for vv in vals]
    den = ex[0] + ex[1] + ex[2] + ex[3]
    for k in range(TOP_K):
        g_ref[k:k + 1, :] = ex[k] / den
    hot_all = jnp.where(hots[0] | hots[1] | hots[2] | hots[3], 1.0, 0.0).astype(F32)
    ts = lax.broadcasted_iota(I32, (tm, tm), 0)
    tt = lax.broadcasted_iota(I32, (tm, tm), 1)
    upper = jnp.where(ts < tt, 1.0, 0.0).astype(BF16)
    before = _nn(hot_all.astype(BF16), upper)
    n_t = jnp.sum(hot_all, axis=1, keepdims=True)
    er = lax.broadcasted_iota(I32, (N_EXPERTS, N_EXPERTS), 0)
    ec = lax.broadcasted_iota(I32, (N_EXPERTS, N_EXPERTS), 1)
    lower = jnp.where(ec < er, 1.0, 0.0).astype(BF16)
    off = _nn(lower, jnp.broadcast_to(n_t, (N_EXPERTS, 128)).astype(BF16))[:, 0:1]
    pos = before + off
    for k in range(TOP_K):
        loc_ref[k:k + 1, :] = jnp.sum(jnp.where(hots[k], pos, 0.0), axis=0, keepdims=True).astype(I32)
    cb_ref[0] = jnp.broadcast_to(cnt_scr[...], (N_EXPERTS, 128)).astype(I32)
    nt_ref[0] = jnp.broadcast_to(n_t, (N_EXPERTS, 128)).astype(I32)
    cnt_scr[...] = cnt_scr[...] + n_t
    cnt_ref[...] = jnp.broadcast_to(cnt_scr[...], cnt_ref.shape).astype(I32)


def _router(h2d, pw, tm):
    n = h2d.shape[0]
    nt = n // tm
    col = lambda i: (0, i)
    per_tile = lambda i: (i, 0, 0)
    return pl.pallas_call(
        _router_kernel,
        grid=(nt,),
        in_specs=[pl.BlockSpec((tm, D_MODEL), lambda i: (i, 0)), _full((1, D_MODEL)), _full((N_EXPERTS, D_MODEL)),
                  _full((N_EXPERTS, 1))],
        out_specs=[pl.BlockSpec((TOP_K, tm), col), pl.BlockSpec((TOP_K, tm), col),
                   pl.BlockSpec((1, N_EXPERTS, 128), per_tile), pl.BlockSpec((1, N_EXPERTS, 128), per_tile),
                   _full((N_EXPERTS, 128))],
        out_shape=[jax.ShapeDtypeStruct((TOP_K, n), I32), jax.ShapeDtypeStruct((TOP_K, n), F32),
                   jax.ShapeDtypeStruct((nt, N_EXPERTS, 128), I32), jax.ShapeDtypeStruct((nt, N_EXPERTS, 128), I32),
                   jax.ShapeDtypeStruct((N_EXPERTS, 128), I32)],
        scratch_shapes=[pltpu.VMEM((N_EXPERTS, 1), F32)],
        compiler_params=_cparams(("arbitrary",), 32),
        name="moe_router",
    )(h2d, pw["norm_ffn"], pw["router_wt"], pw["router_b"])


RUN_CHUNK = 64


def _run_copies(n, src, dst, start_fn):
    nfull = lax.shift_right_logical(n, 6)

    def body(c, carry):
        start_fn(src + c * RUN_CHUNK, dst + c * RUN_CHUNK, RUN_CHUNK)
        return carry

    lax.fori_loop(0, nfull, body, 0)
    base = lax.shift_left(nfull, 6)
    low = n - base
    for b in (32, 16, 8, 4, 2, 1):
        @pl.when(jnp.bitwise_and(low, b) != 0)
        def _():
            o = base + jnp.bitwise_and(low, RUN_CHUNK - 2 * b)
            start_fn(src + o, dst + o, b)


def _dispatch_kernel(rt, pe_ref, pd_ref, h_ref, nf_ref, loc_ref, run_ref, xd_ref, xs3, zbuf, sems, zsem):
    tm = h_ref.shape[0]
    na = TOP_K * tm
    i = pl.program_id(0)
    slot = lax.rem(i, 2)

    @pl.when(i == 0)
    def _():
        zbuf[...] = jnp.zeros_like(zbuf)
        for e in range(N_EXPERTS):
            @pl.when(pd_ref[e] > 0)
            def _():
                st = pl.multiple_of((pe_ref[e] - rt) * 8, rt * 8)
                cp = pltpu.make_async_copy(zbuf, xd_ref.at[pl.ds(st, rt * 8)], zsem)
                cp.start()
                cp.wait()

    def drain(s):
        pltpu.make_async_copy(xs3.at[0], xd_ref.at[pl.ds(0, na * 8)], sems.at[s]).wait()

    xn = _rms(h_ref[...], nf_ref[...]).astype(BF16)
    rows = lax.broadcasted_iota(I32, (na, tm), 0)
    hit = rows == loc_ref[0:1, :]
    for k in range(1, TOP_K):
        hit = hit | (rows == loc_ref[k:k + 1, :])
    xs = _nn(jnp.where(hit, 1.0, 0.0).astype(BF16), xn)

    @pl.when(i > 1)
    def _():
        drain(slot)

    for j in range(D_MODEL // 128):
        xs3[slot, pl.ds(j, na, stride=8), :] = xs[:, 128 * j:128 * (j + 1)]

    def start(sr, ds_, ln):
        pltpu.make_async_copy(xs3.at[slot, pl.ds(pl.multiple_of(sr * 8, 8), ln * 8)],
                              xd_ref.at[pl.ds(pl.multiple_of(ds_ * 8, 8), ln * 8)], sems.at[slot]).start()

    for e in range(N_EXPERTS):
        _run_copies(run_ref[0, e], run_ref[1, e], run_ref[2, e], start)

    @pl.when(i == pl.num_programs(0) - 1)
    def _():
        drain(slot)

        @pl.when(i > 0)
        def _():
            drain(1 - slot)


def _dispatch(h2d, loc, runs, pad_end, padded, cap, rt, pw, tm):
    n = h2d.shape[0]
    gs = pltpu.PrefetchScalarGridSpec(
        num_scalar_prefetch=2,
        grid=(n // tm,),
        in_specs=[pl.BlockSpec((tm, D_MODEL), lambda i, *_: (i, 0)), pl.BlockSpec((1, D_MODEL), lambda i, *_: (0, 0)),
                  pl.BlockSpec((TOP_K, tm), lambda i, *_: (0, i)),
                  pl.BlockSpec((None, 3, N_EXPERTS), lambda i, *_: (i, 0, 0), memory_space=pltpu.SMEM)],
        out_specs=pl.BlockSpec(memory_space=pl.ANY),
        scratch_shapes=[pltpu.VMEM((2, TOP_K * tm * 8, 128), F32), pltpu.VMEM((rt * 8, 128), F32),
                        pltpu.SemaphoreType.DMA((2,)), pltpu.SemaphoreType.DMA],
    )
    return pl.pallas_call(
        functools.partial(_dispatch_kernel, rt),
        grid_spec=gs,
        out_shape=jax.ShapeDtypeStruct((cap * 8, 128), F32),
        compiler_params=_cparams(("arbitrary",), 40),
        name="moe_dispatch",
    )(pad_end, padded, h2d, pw["norm_ffn"], loc, runs)


def _expert_kernel(te_ref, nu_ref, x_ref, wg_ref, bg_ref, wu_ref, bu_ref, wd_ref, bd_ref, y_ref):
    @pl.when(pl.program_id(0) < nu_ref[0])
    def _():
        rt = y_ref.shape[0] // 8
        nj = D_MODEL // 128
        x = jnp.concatenate([x_ref[pl.ds(j, rt, stride=8), :] for j in range(nj)], axis=1).astype(BF16)
        gate = jnp.minimum(_nn(x, wg_ref[0]) + bg_ref[0], SWIGLU_LIMIT)
        up = jnp.clip(_nn(x, wu_ref[0]) + bu_ref[0], -SWIGLU_LIMIT, SWIGLU_LIMIT)
        hid = (up + 1.0) * (gate * _sigmoid(SWIGLU_ALPHA * gate))
        y = _nn(hid.astype(BF16), wd_ref[0]) + bd_ref[0]
        for j in range(nj):
            y_ref[pl.ds(j, rt, stride=8), :] = y[:, 128 * j:128 * (j + 1)]


def _experts(x_disp, tile_expert, n_used, pw, rt):
    cap = x_disp.shape[0] // 8
    d_ff = pw["w_gate"].shape[2]
    tile = lambda i, te, nu: (jnp.minimum(i, nu[0] - 1), 0)
    wsel = lambda i, te, nu: (te[jnp.minimum(i, nu[0] - 1)], 0, 0)
    gs = pltpu.PrefetchScalarGridSpec(
        num_scalar_prefetch=2,
        grid=(cap // rt,),
        in_specs=[pl.BlockSpec((rt * 8, 128), tile),
                  pl.BlockSpec((1, D_MODEL, d_ff), wsel), pl.BlockSpec((1, 1, d_ff), wsel),
                  pl.BlockSpec((1, D_MODEL, d_ff), wsel), pl.BlockSpec((1, 1, d_ff), wsel),
                  pl.BlockSpec((1, d_ff, D_MODEL), wsel), pl.BlockSpec((1, 1, D_MODEL), wsel)],
        out_specs=pl.BlockSpec((rt * 8, 128), tile),
    )
    return pl.pallas_call(
        _expert_kernel,
        grid_spec=gs,
        out_shape=jax.ShapeDtypeStruct((cap * 8, 128), F32),
        compiler_params=_cparams(("arbitrary",), 56),
        name="moe_experts",
    )(tile_expert, n_used, x_disp, pw["w_gate"], pw["b_gate"], pw["w_up"], pw["b_up"], pw["w_down"], pw["b_down"])


def _combine_kernel(h_ref, g_ref, loc_ref, nfin_ref, run_ref, runn_ref, yd_ref, o_ref, ys3, sems):
    tm = h_ref.shape[0]
    na = TOP_K * tm
    i = pl.program_id(0)
    slot = lax.rem(i, 2)

    def issue(rref, s):
        def start(sr, ds_, ln):
            pltpu.make_async_copy(yd_ref.at[pl.ds(pl.multiple_of(ds_ * 8, 8), ln * 8)],
                                  ys3.at[s, pl.ds(pl.multiple_of(sr * 8, 8), ln * 8)], sems.at[s]).start()
        for e in range(N_EXPERTS):
            _run_copies(rref[0, e], rref[1, e], rref[2, e], start)

    @pl.when(i == 0)
    def _():
        issue(run_ref, 0)

    @pl.when(i + 1 < pl.num_programs(0))
    def _():
        issue(runn_ref, 1 - slot)

    pltpu.make_async_copy(yd_ref.at[pl.ds(0, na * 8)], ys3.at[0], sems.at[slot]).wait()
    ys = jnp.concatenate([ys3[slot, pl.ds(j, na, stride=8), :] for j in range(D_MODEL // 128)], axis=1).astype(BF16)
    cols = lax.broadcasted_iota(I32, (tm, na), 1)
    g = g_ref[...]
    loc = loc_ref[...]
    gm = jnp.where(cols == loc[:, 0:1], g[:, 0:1], 0.0)
    for k in range(1, TOP_K):
        gm = gm + jnp.where(cols == loc[:, k:k + 1], g[:, k:k + 1], 0.0)
    o_ref[...] = _rms(h_ref[...] + _nn(gm.astype(BF16), ys), nfin_ref[...])


def _combine(h2d, gates_t, loc_t, runs, y_disp, pw, tm):
    n = h2d.shape[0]
    nt = n // tm
    row = lambda i: (i, 0)
    return pl.pallas_call(
        _combine_kernel,
        grid=(nt,),
        in_specs=[pl.BlockSpec((tm, D_MODEL), row), pl.BlockSpec((tm, TOP_K), row), pl.BlockSpec((tm, TOP_K), row),
                  _full((1, D_MODEL)),
                  pl.BlockSpec((None, 3, N_EXPERTS), lambda i: (i, 0, 0), memory_space=pltpu.SMEM),
                  pl.BlockSpec((None, 3, N_EXPERTS), lambda i: (jnp.minimum(i + 1, nt - 1), 0, 0),
                               memory_space=pltpu.SMEM),
                  pl.BlockSpec(memory_space=pl.ANY)],
        out_specs=pl.BlockSpec((tm, D_MODEL), row),
        out_shape=jax.ShapeDtypeStruct((n, D_MODEL), F32),
        scratch_shapes=[pltpu.VMEM((2, TOP_K * tm * 8, 128), F32), pltpu.SemaphoreType.DMA((2,))],
        compiler_params=_cparams(("arbitrary",), 40),
        name="moe_combine",
    )(h2d, gates_t, loc_t, pw["norm_final"], runs, runs, y_disp)


def _moe_and_final_norm(h2d, pw, rt, tm):
    n = h2d.shape[0]
    loc, gates, cnt_before, n_tile, counts = _router(h2d, pw, tm)
    counts = counts[:, 0]
    padded = (counts + rt - 1) // rt * rt
    pad_end = jnp.cumsum(padded).astype(I32)
    pad_start = pad_end - padded
    n_tiles = (n * TOP_K) // rt + N_EXPERTS
    cap = n_tiles * rt
    n_te = n_tile[:, :, 0]
    runs = jnp.stack([n_te, jnp.cumsum(n_te, axis=1) - n_te, pad_start[None, :] + cnt_before[:, :, 0]], axis=1).astype(I32)
    tile_expert = jnp.minimum(jnp.sum((pad_end[None, :] <= (jnp.arange(n_tiles, dtype=I32) * rt)[:, None]).astype(I32),
                                      axis=1), N_EXPERTS - 1).astype(I32)
    n_used = (pad_end[-1:] // rt).astype(I32)
    x_disp = _dispatch(h2d, loc, runs, pad_end, padded.astype(I32), cap, rt, pw, tm)
    y_disp = _experts(x_disp, tile_expert, n_used, pw, rt)
    return _combine(h2d, gates.T, loc.T, runs, y_disp, pw, tm)


def _prep_weights(norm_mix, w_in, mu_shift, rw_w0, rw_w2, rw_a0, rw_a2, rw_g2, rw_k_k, rw_k_a, rw_r_k, rw_lnx_w,
                  rw_lnx_b, mla_q_norm, mla_w_qb, mla_kv_norm, mla_w_kvb, mla_out_norm, w_out, norm_x, norm_mem,
                  xa_wq, xa_wk, xa_wv, xa_wo, norm_ffn, router_w, router_b, moe_w_gate, moe_b_gate, moe_w_up,
                  moe_b_up, moe_w_down, moe_b_down, norm_final):
    w_m = w_in[:, RW_PROJ:]
    qb = mla_w_qb.reshape(Q_RANK, MLA_HEADS, QK_NOPE + QK_ROPE)
    w_kv = mla_w_kvb.reshape(KV_RANK, MLA_HEADS, QK_NOPE + V_HEAD)
    z64 = jnp.zeros((64, RW_WIDTH), F32)
    blk = jnp.arange(RW_WIDTH) // RW_HEAD_DIM
    return {
        "norm_mix": norm_mix.reshape(1, -1),
        "w_r": w_in[:, :RW_PROJ].astype(BF16),
        "w_m": jnp.concatenate([w_m[:, :640], jnp.tile(w_m[:, 640:672], (1, 4)), jnp.tile(w_m[:, 672:704], (1, 4))],
                               axis=1).astype(BF16),
        "q_norm": mla_q_norm.reshape(1, -1),
        "w_qb": jnp.concatenate([qb[:, :, :QK_NOPE].reshape(Q_RANK, -1), qb[:, :, QK_NOPE:QK_NOPE + 32].reshape(Q_RANK, -1),
                                 qb[:, :, QK_NOPE + 32:].reshape(Q_RANK, -1)], axis=1).astype(BF16),
        "kv_norm": mla_kv_norm.reshape(1, -1),
        "wk": jnp.transpose(w_kv[:, :, :QK_NOPE], (1, 2, 0)).astype(BF16),
        "wv": jnp.transpose(w_kv[:, :, QK_NOPE:], (1, 0, 2)).astype(BF16),
        "out_norm": mla_out_norm.reshape(1, -1),
        "w_out": w_out.astype(BF16),
        "mu": mu_shift.reshape(1, -1),
        "rw_vec": jnp.stack([rw_w0, rw_a0, rw_k_k, rw_k_a, rw_r_k.reshape(-1), rw_lnx_w, rw_lnx_b,
                             jnp.zeros_like(rw_w0)], axis=0),
        "w2a2": jnp.stack([jnp.concatenate([rw_w2, z64], axis=0), jnp.concatenate([z64, rw_a2], axis=0)]).astype(BF16),
        "g2": rw_g2.astype(BF16),
        "ones64": (blk[:, None] == blk[None, :]).astype(BF16),
        "norm_x": norm_x.reshape(1, -1),
        "norm_mem": norm_mem.reshape(1, -1),
        "xa_wq": xa_wq.astype(BF16), "xa_wk": xa_wk.astype(BF16), "xa_wv": xa_wv.astype(BF16),
        "xa_wo": xa_wo.astype(BF16),
        "norm_ffn": norm_ffn.reshape(1, -1),
        "router_wt": router_w.T,
        "router_b": router_b.reshape(-1, 1),
        "w_gate": moe_w_gate.astype(BF16), "b_gate": moe_b_gate[:, None, :],
        "w_up": moe_w_up.astype(BF16), "b_up": moe_b_up[:, None, :],
        "w_down": moe_w_down.astype(BF16), "b_down": moe_b_down[:, None, :],
        "norm_final": norm_final.reshape(1, -1),
    }


def _pick(n, prefs):
    for p in prefs:
        if n % p == 0:
            return p
    return n


def kernel(x_prompt, x_sample, mem_prompt, cache_ckv, cache_krope, cache_mem_k, cache_mem_v, state_rwkv, state_shift, page_table, norm_mix, w_in, mu_shift, rw_w0, rw_w2, rw_a0, rw_a2, rw_g2, rw_k_k, rw_k_a, rw_r_k, rw_lnx_w, rw_lnx_b, mla_q_norm, mla_w_qb, mla_kv_norm, mla_w_kvb, mla_out_norm, w_out, norm_x, norm_mem, xa_wq, xa_wk, xa_wv, xa_wo, norm_ffn, router_w, router_b, moe_w_gate, moe_b_gate, moe_w_up, moe_b_up, moe_w_down, moe_b_down, norm_final):
    assert w_in.shape[0] == 1, "single-layer trunk"
    bp, tp, _ = x_prompt.shape
    bs, ts, _ = x_sample.shape
    n_pages = page_table.shape[1]
    past_len = n_pages * PAGE_SIZE
    assert ts <= RW_CHUNK and tp % RW_CHUNK == 0
    pw = _prep_weights(norm_mix[0], w_in[0], mu_shift[0], rw_w0[0], rw_w2[0], rw_a0[0], rw_a2[0], rw_g2[0],
                       rw_k_k[0], rw_k_a[0], rw_r_k[0], rw_lnx_w[0], rw_lnx_b[0], mla_q_norm[0], mla_w_qb[0],
                       mla_kv_norm[0], mla_w_kvb[0], mla_out_norm[0], w_out[0], norm_x[0], norm_mem[0], xa_wq[0],
                       xa_wk[0], xa_wv[0], xa_wo[0], norm_ffn[0], router_w[0], router_b[0], moe_w_gate[0],
                       moe_b_gate[0], moe_w_up[0], moe_b_up[0], moe_w_down[0], moe_b_down[0], norm_final)
    np_, ns_ = bp * tp, bs * ts
    rt = _pick(ns_ * TOP_K, (512, 256, 128))

    xp2 = x_prompt.reshape(np_, D_MODEL)
    projr_p, qcat_p, kcat_p, ckv_p, krope_p = _mix_in(xp2, tp, 0, pw, _pick(tp, (256, 128, 64)))
    yrw_p, rwkv_p = _rwkv(projr_p.reshape(bp, tp, RW_PROJ), jnp.zeros((bp, RW_PROJ), F32),
                          jnp.zeros((bp, RW_HEADS, 64, 64), F32), pw, RW_CHUNK, _pick(tp, (256, 128, 64)),
                          _pick(tp, (256, 128, 64)) // RW_CHUNK)
    tq = _pick(tp, (256, 128, 64))
    h1_p = _mla_prompt(qcat_p, kcat_p.reshape(bp, tp, KCAT), xp2, yrw_p.reshape(np_, RW_WIDTH), pw, tq,
                       _pick(tp, (512, 256, 128, 64)))
    mk2, mv2 = _mem_kv(mem_prompt.reshape(bp * MEM_TOKENS, D_MODEL), pw, _pick(bp * MEM_TOKENS, (512, 256)))
    h2_p = _xattn(h1_p, mk2.reshape(bp, MEM_TOKENS, X_WIDTH), mv2.reshape(bp, MEM_TOKENS, X_WIDTH), tp, pw,
                  _pick(tp, (512, 256, 128, 64)))
    y_p = _moe_and_final_norm(h2_p, pw, rt, _pick(np_, (256, 128)))

    xs2 = x_sample.reshape(ns_, D_MODEL)
    projr_s, qcat_s, _, ckv_s, krope_s = _mix_in(xs2, ts, past_len, pw, _pick(ns_, (256, 128, 64, 8)))
    projr_s3 = projr_s.reshape(bs, ts, RW_PROJ)
    projr_pad = jnp.pad(projr_s3, ((0, 0), (0, RW_CHUNK - ts), (0, 0)))
    yrw_s, rwkv_s = _rwkv(projr_pad, state_shift[0], state_rwkv[0], pw, ts, RW_CHUNK, 1, _pick(bs, (4, 2, 1)))
    yrw_s = yrw_s[:, :ts].reshape(ns_, RW_WIDTH)
    q4 = qcat_s.reshape(bs, ts, MLA_HEADS, KCAT).transpose(0, 2, 1, 3)
    q_lat = q4[..., :KV_RANK].reshape(bs, MLA_HEADS * ts, KV_RANK)
    o1 = q4[..., KV_RANK:KV_RANK + 128].reshape(bs, MLA_HEADS, ts, MLA_HEADS, 32)
    o2 = q4[..., KV_RANK + 128:].reshape(bs, MLA_HEADS, ts, MLA_HEADS, 32)
    q_rope = jnp.stack([jnp.concatenate([o1[:, h, :, h], o2[:, h, :, h]], axis=-1) for h in range(MLA_HEADS)],
                       axis=1).reshape(bs, MLA_HEADS * ts, QK_ROPE)
    new_ckv = jnp.pad(ckv_s.reshape(bs, ts, KV_RANK), ((0, 0), (0, PAGE_SIZE - ts), (0, 0)))
    new_kr = jnp.pad(krope_s.reshape(bs, ts, QK_ROPE), ((0, 0), (0, PAGE_SIZE - ts), (0, 0)))
    o_s = _paged_attn(q_lat, q_rope, new_ckv, new_kr, cache_ckv, jnp.swapaxes(cache_krope, 2, 3), page_table, ts,
                      _pick(n_pages, (64, 32, 16, 8, 4, 2, 1)))
    o_s = o_s.reshape(bs, MLA_HEADS, ts, KV_RANK).transpose(0, 2, 1, 3).reshape(ns_, MLA_HEADS * KV_RANK)
    h1_s = _out_proj_call(o_s, xs2, yrw_s, pw, _pick(ns_, (256, 128, 64, 8)))
    h2_s = _xattn(h1_s, cache_mem_k[0].reshape(bs, MEM_TOKENS, X_WIDTH), cache_mem_v[0].reshape(bs, MEM_TOKENS, X_WIDTH),
                  ts, pw, ts)
    y_s = _moe_and_final_norm(h2_s, pw, rt, _pick(ns_, (256, 128)))

    return (y_p.reshape(bp, tp, D_MODEL), y_s.reshape(bs, ts, D_MODEL),
            ckv_p.reshape(1, bp, tp, KV_RANK), krope_p.reshape(1, bp, tp, QK_ROPE),
            mk2.reshape(1, bp, MEM_TOKENS, X_HEADS, X_HEAD_DIM), mv2.reshape(1, bp, MEM_TOKENS, X_HEADS, X_HEAD_DIM),
            rwkv_p[None], projr_p.reshape(bp, tp, RW_PROJ)[None, :, -1],
            ckv_s.reshape(1, bs, ts, KV_RANK), krope_s.reshape(1, bs, ts, QK_ROPE),
            rwkv_s[None], projr_s3[None, :, -1])
```
